```python
import math
import jax, jax.numpy as jnp
from jax import lax
import numpy as np

D_MODEL = 1024
BATCH = 8
SEQ = 4096
DEPTH = 2

CHUNK = 64
EPS = 1e-6
A_HEADS = 8
A_HEAD_DIM = 64
A_WIDTH = A_HEADS * A_HEAD_DIM
KV_RANK = 128
IDX_HEADS = 4
IDX_DIM = 64
TOPK_MAX = 256
Q_BLOCK = 128
S5_WIDTH = 512
S5_GROUP = 16
S5_GROUPS = S5_WIDTH // S5_GROUP
S5_STATE = 64
DT_MIN = 0.001
DT_MAX = 0.1
C_HEADS = 4
C_HEAD_DIM = 128
C_WIDTH = C_HEADS * C_HEAD_DIM
C_CONV = 4
D_FF = 2816
FFN_CONV = 3
N_BRANCH = 3
IN_SPLITS = (A_WIDTH, KV_RANK, IDX_HEADS * IDX_DIM, IDX_DIM, IDX_HEADS,
             S5_WIDTH,
             3 * C_WIDTH, C_WIDTH, C_HEADS, C_HEADS,
             N_BRANCH * D_MODEL)
N_IN = (A_WIDTH + KV_RANK + IDX_HEADS * IDX_DIM + IDX_DIM + IDX_HEADS + S5_WIDTH
        + 3 * C_WIDTH + C_WIDTH + 2 * C_HEADS + N_BRANCH * D_MODEL)

kernel_name = 'hybrid_dsa_s5_gdn_encoder'


def rms_norm(x, g):
    xf = x.astype(jnp.float32)
    y = xf * lax.rsqrt(jnp.mean(xf * xf, axis=-1, keepdims=True) + EPS)
    return (y * g.astype(jnp.float32)).astype(x.dtype)


def l2_norm(x):
    return x * lax.rsqrt(jnp.sum(x * x, axis=-1, keepdims=True) + EPS)


def causal_dwconv(x, w):
    k = w.shape[0]
    t = x.shape[1]
    xp = jnp.pad(x, ((0, 0), (k - 1, 0), (0, 0)))
    out = xp[:, 0:t] * w[0]
    for j in range(1, k):
        out = out + xp[:, j:j + t] * w[j]
    return out


def dsa_attention(q, c_kv, iq, ik, iw, kv_norm_g, w_uk, w_uv):
    b, t = q.shape[0], q.shape[1]
    topk = min(TOPK_MAX, t // 4)
    nb = t // Q_BLOCK
    c_kv = rms_norm(c_kv, kv_norm_g)
    q_lat = jnp.einsum('bthd,rhd->bthr', q, w_uk) * (A_HEAD_DIM ** -0.5)
    iw = iw * (IDX_HEADS ** -0.5)
    key_pos = jnp.arange(t)

    def to_blocks(a):
        a = a.reshape((b, nb, Q_BLOCK) + a.shape[2:])
        return jnp.moveaxis(a, 1, 0)

    def block(args):
        qb, iqb, iwb, pos = args
        limit = (pos // CHUNK + 1) * CHUNK
        adm = key_pos[None, :] < limit[:, None]
        rel = jax.nn.relu(jnp.einsum('bqhd,bsd->bqhs', iqb, ik) * (IDX_DIM ** -0.5))
        score = jnp.einsum('bqhs,bqh->bqs', rel, iwb).astype(jnp.float32)
        score = jnp.where(adm[None], score, -jnp.inf)
        _, idx = lax.top_k(score, topk)
        valid = idx < limit[None, :, None]
        c_sel = jax.vmap(lambda c, i: c[i])(c_kv, idx)
        logits = jnp.einsum('bqhr,bqkr->bqhk', qb, c_sel).astype(jnp.float32)
        logits = jnp.where(valid[:, :, None, :], logits, -jnp.inf)
        p = jax.nn.softmax(logits, axis=-1).astype(c_sel.dtype)
        o_lat = jnp.einsum('bqhk,bqkr->bqhr', p, c_sel)
        return jnp.einsum('bqhr,rhd->bqhd', o_lat, w_uv)

    pos_blocks = key_pos.reshape(nb, Q_BLOCK)
    out = lax.map(block, (to_blocks(q_lat), to_blocks(iq), to_blocks(iw), pos_blocks))
    return jnp.moveaxis(out, 0, 1).reshape(b, t, A_WIDTH)


def _complex_affine_combine(e1, e2):
    a1r, a1i, b1r, b1i = e1
    a2r, a2i, b2r, b2i = e2
    return (a1r * a2r - a1i * a2i,
            a1r * a2i + a1i * a2r,
            a2r * b1r - a2i * b1i + b2r,
            a2r * b1i + a2i * b1r + b2i)


def s5_branch(u, a_re, a_im, log_dt, b_re, b_im, c_re, c_im, d, w_glu):
    bsz, t = u.shape[0], u.shape[1]
    f32 = jnp.float32
    uf = u.astype(f32).reshape(bsz, t, S5_GROUPS, S5_GROUP)
    a_re, a_im = a_re.astype(f32), a_im.astype(f32)
    b_re, b_im = b_re.astype(f32), b_im.astype(f32)
    c_re, c_im = c_re.astype(f32), c_im.astype(f32)
    dt = jnp.exp(log_dt.astype(f32))[:, None]
    mag = jnp.exp(dt * a_re)
    ang = dt * a_im
    ab_re = mag * jnp.cos(ang)
    ab_im = mag * jnp.sin(ang)
    den = a_re * a_re + a_im * a_im
    n_re = ab_re - 1.0
    f_re = (n_re * a_re + ab_im * a_im) / den
    f_im = (ab_im * a_re - n_re * a_im) / den
    bb_re = f_re[..., None] * b_re - f_im[..., None] * b_im
    bb_im = f_re[..., None] * b_im + f_im[..., None] * b_re
    x_re = jnp.einsum('btgm,gpm->btgp', uf, bb_re)
    x_im = jnp.einsum('btgm,gpm->btgp', uf, bb_im)
    a_shape = (1, t) + ab_re.shape
    elems = (jnp.broadcast_to(ab_re, a_shape), jnp.broadcast_to(ab_im, a_shape), x_re, x_im)
    _, _, h_re, h_im = lax.associative_scan(_complex_affine_combine, elems, axis=1)
    y = (jnp.einsum('btgp,gmp->btgm', h_re, c_re) - jnp.einsum('btgp,gmp->btgm', h_im, c_im)
         + d.astype(f32).reshape(S5_GROUPS, S5_GROUP) * uf)
    y = jax.nn.gelu(y.reshape(bsz, t, S5_WIDTH).astype(u.dtype))
    return y * jax.nn.sigmoid(y @ w_glu)


def gated_delta_rule(q, k, v, g, beta):
    bsz, t, h, dk = q.shape
    dv = v.shape[-1]
    n = t // CHUNK

    def chunks(a):
        a = jnp.moveaxis(a, 2, 1)
        return a.reshape((bsz, h, n, CHUNK) + a.shape[3:])

    q = chunks(q) * (dk ** -0.5)
    k, v, g, beta = chunks(k), chunks(v), chunks(g), chunks(beta)
    gam = jnp.cumsum(g, axis=-1)
    i = jnp.arange(CHUNK)
    lower_incl = i[:, None] >= i[None, :]
    strict = i[:, None] > i[None, :]
    decay = jnp.exp(jnp.where(lower_incl, gam[..., :, None] - gam[..., None, :], -jnp.inf))
    kb = k * beta[..., None]
    m = jnp.where(strict, jnp.einsum('bhncd,bhnsd->bhncs', kb, k) * decay, 0.0)
    eye = jnp.eye(CHUNK, dtype=m.dtype)
    rhs = jnp.concatenate([v * beta[..., None], kb * jnp.exp(gam)[..., None]], axis=-1)
    sol = lax.linalg.triangular_solve(m + eye, rhs, left_side=True, lower=True, unit_diagonal=True)
    u, w = sol[..., :dv], sol[..., dv:]
    attn = jnp.einsum('bhncd,bhnsd->bhncs', q, k) * decay
    q_dec = q * jnp.exp(gam)[..., None]
    k_dec = k * jnp.exp(gam[..., -1:] - gam)[..., None]
    last = jnp.exp(gam[..., -1])

    def step(s, xs):
        u_c, w_c, a_c, qd_c, kd_c, l_c = xs
        v_new = u_c - jnp.einsum('bhcd,bhde->bhce', w_c, s)
        o = jnp.einsum('bhcd,bhde->bhce', qd_c, s) + jnp.einsum('bhcs,bhse->bhce', a_c, v_new)
        s = s * l_c[..., None, None] + jnp.einsum('bhcd,bhce->bhde', kd_c, v_new)
        return s, o

    xs = tuple(jnp.moveaxis(a, 2, 0) for a in (u, w, attn, q_dec, k_dec, last))
    s0 = jnp.zeros((bsz, h, dk, dv), q.dtype)
    _, o = lax.scan(step, s0, xs)
    o = jnp.moveaxis(o, 0, 2).reshape(bsz, h, t, dv)
    return jnp.moveaxis(o, 1, 2)


def gdn_branch(qkv, z, a, bgate, conv_w, a_log, dt_bias, norm_g):
    bsz, t = qkv.shape[0], qkv.shape[1]
    dtype = qkv.dtype
    f32 = jnp.float32
    qkv = jax.nn.silu(causal_dwconv(qkv, conv_w)).astype(f32)
    q, k, v = jnp.split(qkv, 3, axis=-1)
    shp = (bsz, t, C_HEADS, C_HEAD_DIM)
    q = l2_norm(q.reshape(shp))
    k = l2_norm(k.reshape(shp))
    v = v.reshape(shp)
    beta = jax.nn.sigmoid(bgate.astype(f32))
    g = -jnp.exp(a_log.astype(f32)) * jax.nn.softplus(a.astype(f32) + dt_bias.astype(f32))
    o = gated_delta_rule(q, k, v, g, beta)
    o = rms_norm(o, norm_g) * jax.nn.silu(z.astype(f32).reshape(shp))
    return o.reshape(bsz, t, C_WIDTH).astype(dtype)


def conv_glu_ffn(h, w_up, conv_w, w_down):
    hid = causal_dwconv(h @ w_up, conv_w)
    gate, val = jnp.split(hid, 2, axis=-1)
    return (jax.nn.silu(gate) * val) @ w_down


def setup_inputs(seed: int = 0) -> dict:
    key = jax.random.key(seed)
    ks = iter(jax.random.split(key, 32))
    f32 = jnp.float32

    def nrm(shape, scale):
        return jax.random.normal(next(ks), shape, f32) * scale

    L, D = DEPTH, D_MODEL
    G, P, M = S5_GROUPS, S5_STATE, S5_GROUP
    x = nrm((BATCH, SEQ, D), 1.0)
    attn_norm_g = 1.0 + nrm((L, D), 0.01)
    w_in = nrm((L, D, N_IN), D ** -0.5)
    kv_norm_g = 1.0 + nrm((L, KV_RANK), 0.01)
    w_uk = nrm((L, KV_RANK, A_HEADS, A_HEAD_DIM), KV_RANK ** -0.5)
    w_uv = nrm((L, KV_RANK, A_HEADS, A_HEAD_DIM), KV_RANK ** -0.5)
    w_proj_a = nrm((L, A_WIDTH, D), A_WIDTH ** -0.5)
    s5_a_re = -0.5 + nrm((L, G, P), 0.01)
    s5_a_im = math.pi * jnp.arange(P, dtype=f32) + nrm((L, G, P), 0.01)
    s5_log_dt = jax.random.uniform(next(ks), (L, G), f32, math.log(DT_MIN), math.log(DT_MAX))
    s5_b_re = nrm((L, G, P, M), (2 * M) ** -0.5)
    s5_b_im = nrm((L, G, P, M), (2 * M) ** -0.5)
    s5_c_re = nrm((L, G, M, P), 0.5)
    s5_c_im = nrm((L, G, M, P), 0.5)
    s5_d = nrm((L, S5_WIDTH), 1.0)
    w_glu = nrm((L, S5_WIDTH, S5_WIDTH), S5_WIDTH ** -0.5)
    w_proj_b = nrm((L, S5_WIDTH, D), S5_WIDTH ** -0.5)
    gdn_conv_w = nrm((L, C_CONV, 3 * C_WIDTH), C_CONV ** -0.5)
    gdn_a_log = jnp.log(jax.random.uniform(next(ks), (L, C_HEADS), f32, 1.0, 16.0))
    dt = jnp.exp(jax.random.uniform(next(ks), (L, C_HEADS), f32, math.log(DT_MIN), math.log(DT_MAX)))
    gdn_dt_bias = dt + jnp.log(-jnp.expm1(-dt))
    gdn_norm_g = 1.0 + nrm((L, C_HEAD_DIM), 0.01)
    w_proj_c = nrm((L, C_WIDTH, D), C_WIDTH ** -0.5)
    w_out = nrm((L, D, D), D ** -0.5)
    ffn_norm_g = 1.0 + nrm((L, D), 0.01)
    w_up = nrm((L, D, 2 * D_FF), D ** -0.5)
    ffn_conv_w = nrm((L, FFN_CONV, 2 * D_FF), FFN_CONV ** -0.5)
    w_down = nrm((L, D_FF, D), D_FF ** -0.5)
    final_norm_g = 1.0 + nrm((D,), 0.01)
    return {'x': x, 'attn_norm_g': attn_norm_g, 'w_in': w_in,
            'kv_norm_g': kv_norm_g, 'w_uk': w_uk, 'w_uv': w_uv, 'w_proj_a': w_proj_a,
            's5_a_re': s5_a_re, 's5_a_im': s5_a_im, 's5_log_dt': s5_log_dt,
            's5_b_re': s5_b_re, 's5_b_im': s5_b_im, 's5_c_re': s5_c_re, 's5_c_im': s5_c_im,
            's5_d': s5_d, 'w_glu': w_glu, 'w_proj_b': w_proj_b,
            'gdn_conv_w': gdn_conv_w, 'gdn_a_log': gdn_a_log, 'gdn_dt_bias': gdn_dt_bias,
            'gdn_norm_g': gdn_norm_g, 'w_proj_c': w_proj_c,
            'w_out': w_out, 'ffn_norm_g': ffn_norm_g, 'w_up': w_up, 'ffn_conv_w': ffn_conv_w,
            'w_down': w_down, 'final_norm_g': final_norm_g}


def reference(x, attn_norm_g, w_in, kv_norm_g, w_uk, w_uv, w_proj_a,
              s5_a_re, s5_a_im, s5_log_dt, s5_b_re, s5_b_im, s5_c_re, s5_c_im, s5_d, w_glu, w_proj_b,
              gdn_conv_w, gdn_a_log, gdn_dt_bias, gdn_norm_g, w_proj_c,
              w_out, ffn_norm_g, w_up, ffn_conv_w, w_down, final_norm_g):
    bsz, t = x.shape[0], x.shape[1]
    split_points = [int(s) for s in np.cumsum(IN_SPLITS)[:-1]]
    for l in range(DEPTH):
        h = rms_norm(x, attn_norm_g[l])
        (a_q, a_ckv, a_iq, a_ik, a_iw, s5_u, c_qkv, c_z, c_a, c_b,
         gates) = jnp.split(h @ w_in[l], split_points, axis=-1)
        y_a = dsa_attention(a_q.reshape(bsz, t, A_HEADS, A_HEAD_DIM), a_ckv,
                            a_iq.reshape(bsz, t, IDX_HEADS, IDX_DIM), a_ik, a_iw,
                            kv_norm_g[l], w_uk[l], w_uv[l]) @ w_proj_a[l]
        y_b = s5_branch(s5_u, s5_a_re[l], s5_a_im[l], s5_log_dt[l], s5_b_re[l], s5_b_im[l],
                        s5_c_re[l], s5_c_im[l], s5_d[l], w_glu[l]) @ w_proj_b[l]
        y_c = gdn_branch(c_qkv, c_z, c_a, c_b, gdn_conv_w[l], gdn_a_log[l], gdn_dt_bias[l],
                         gdn_norm_g[l]) @ w_proj_c[l]
        g_a, g_b, g_c = jnp.split(jax.nn.sigmoid(gates), N_BRANCH, axis=-1)
        x = x + (g_a * y_a + g_b * y_b + g_c * y_c) @ w_out[l]
        x = x + conv_glu_ffn(rms_norm(x, ffn_norm_g[l]), w_up[l], ffn_conv_w[l], w_down[l])
    return rms_norm(x, final_norm_g)
```

```python
import functools
import math

import numpy as np
import jax
import jax.numpy as jnp
from jax import lax
from jax.experimental import pallas as pl
from jax.experimental.pallas import tpu as pltpu

F32 = jnp.float32
BF16 = jnp.bfloat16
I32 = jnp.int32

D_MODEL = 1024
EPS = 1e-6
CHUNK = 64
A_HEADS, A_HEAD_DIM, KV_RANK = 8, 64, 128
A_WIDTH = A_HEADS * A_HEAD_DIM
IDX_HEADS, IDX_DIM = 4, 64
TOPK_MAX = 256
S5_WIDTH, S5_GROUP, S5_STATE = 512, 16, 64
S5_GROUPS = S5_WIDTH // S5_GROUP
C_HEADS, C_HEAD_DIM, C_CONV = 4, 128, 4
C_WIDTH = C_HEADS * C_HEAD_DIM
D_FF, FFN_CONV = 2816, 3
N_BRANCH = 3
IN_SPLITS = (A_WIDTH, KV_RANK, IDX_HEADS * IDX_DIM, IDX_DIM, IDX_HEADS, S5_WIDTH,
             3 * C_WIDTH, C_WIDTH, C_HEADS, C_HEADS, N_BRANCH * D_MODEL)

LANE = 128
SUBLANE = 8
VMEM_LIMIT = 56 * 1024 * 1024
MASKED = -1e30
INT_MIN = -2 ** 31

WT_Q, WT_IQ, WT_CKV, WT_SMALL = 0, 512, 1024, 1152
WT_ROWS = 1168
SM_IW, SM_GA = 0, 8
WN_CKV, WN_IK, WN_S5, WN_QKV, WN_Z, WN_AB, WN_GATE, WN_COLS = 0, 128, 256, 768, 2304, 2816, 2944, 6016


def _dot(a, b):
    return jnp.dot(a, b, preferred_element_type=F32)


def _dot_nt(a, b):
    return lax.dot_general(a, b, (((1,), (1,)), ((), ())), preferred_element_type=F32)


def _dot_f32(a, b):
    return jnp.dot(a, b, preferred_element_type=F32, precision=lax.Precision.HIGHEST)


def _rms(x, g):
    return x * lax.rsqrt(jnp.mean(x * x, axis=-1, keepdims=True) + EPS) * g


def _silu(x):
    return x * jax.nn.sigmoid(x)


def _const_spec(shape):
    nd = len(shape)
    return pl.BlockSpec(shape, lambda *_: (0,) * nd, pipeline_mode=pl.Buffered(1))


def _params(*sem):
    return pltpu.CompilerParams(dimension_semantics=sem, vmem_limit_bytes=VMEM_LIMIT)


def _in_proj_body(x_ref, g_ref, wt_ref, wn_ref, kvg_col_ref, kvg_row_ref,
                  qT_ref, iqT_ref, ckvT_ref, smT_ref, ckv_ref, ik_ref, u_ref, qkv_ref, z_ref, ab_ref,
                  gate_ref, *, tm):
    hb = _rms(x_ref[...], g_ref[...]).astype(BF16)
    qT_ref[...] = _dot_nt(wt_ref[WT_Q:WT_IQ, :], hb).astype(BF16)
    iqT_ref[...] = _dot_nt(wt_ref[WT_IQ:WT_CKV, :], hb).astype(BF16)
    cT = _dot_nt(wt_ref[WT_CKV:WT_SMALL, :], hb)
    cT = cT * lax.rsqrt(jnp.mean(cT * cT, axis=0, keepdims=True) + EPS) * kvg_col_ref[...]
    for j in range(tm // LANE):
        ckvT_ref[j] = cT[:, j * LANE:(j + 1) * LANE].astype(BF16)
    smT_ref[...] = _dot_nt(wt_ref[WT_SMALL:WT_ROWS, :], hb)
    c = _dot(hb, wn_ref[:, WN_CKV:WN_IK])
    ckv_ref[...] = _rms(c, kvg_row_ref[...]).astype(BF16)
    ik_ref[...] = _dot(hb, wn_ref[:, WN_IK:WN_S5]).astype(BF16)
    u_ref[...] = _dot(hb, wn_ref[:, WN_S5:WN_QKV]).astype(BF16)
    for j in range(3):
        lo = WN_QKV + j * C_WIDTH
        qkv_ref[:, j * C_WIDTH:(j + 1) * C_WIDTH] = _dot(hb, wn_ref[:, lo:lo + C_WIDTH]).astype(BF16)
    z_ref[...] = _dot(hb, wn_ref[:, WN_Z:WN_AB]).astype(BF16)
    ab_ref[...] = _dot(hb, wn_ref[:, WN_AB:WN_GATE])
    for j in range(2 * N_BRANCH):
        lo = WN_GATE + j * 512
        gate_ref[:, j * 512:(j + 1) * 512] = _dot(hb, wn_ref[:, lo:lo + 512]).astype(BF16)


def _in_proj(x, g, wt, wn, kvg, tm=256):
    b, t, d = x.shape
    nt = t // tm
    row = lambda w: pl.BlockSpec((None, tm, w), lambda bi, i: (bi, i, 0))
    col = lambda r: pl.BlockSpec((None, r, tm), lambda bi, i: (bi, 0, i))
    out_shape = (
        jax.ShapeDtypeStruct((b, 512, t), BF16),
        jax.ShapeDtypeStruct((b, 512, t), BF16),
        jax.ShapeDtypeStruct((b, t // LANE, KV_RANK, LANE), BF16),
        jax.ShapeDtypeStruct((b, 16, t), F32),
        jax.ShapeDtypeStruct((b, t, KV_RANK), BF16),
        jax.ShapeDtypeStruct((b, t, LANE), BF16),
        jax.ShapeDtypeStruct((b, t, S5_WIDTH), BF16),
        jax.ShapeDtypeStruct((b, t, 3 * C_WIDTH), BF16),
        jax.ShapeDtypeStruct((b, t, C_WIDTH), BF16),
        jax.ShapeDtypeStruct((b, t, LANE), F32),
        jax.ShapeDtypeStruct((b, t, N_BRANCH * D_MODEL), BF16),
    )
    out_specs = (
        col(512), col(512),
        pl.BlockSpec((None, tm // LANE, KV_RANK, LANE), lambda bi, i: (bi, i, 0, 0)),
        col(16), row(KV_RANK), row(LANE), row(S5_WIDTH), row(3 * C_WIDTH), row(C_WIDTH), row(LANE),
        row(N_BRANCH * D_MODEL),
    )
    return pl.pallas_call(
        functools.partial(_in_proj_body, tm=tm),
        grid=(b, nt),
        in_specs=[row(d), _const_spec((1, d)), _const_spec(wt.shape), _const_spec(wn.shape),
                  _const_spec((KV_RANK, 1)), _const_spec((1, KV_RANK))],
        out_specs=out_specs, out_shape=out_shape,
        compiler_params=_params("parallel", "parallel"), name="in_proj",
    )(x, g.reshape(1, d), wt, wn, kvg.reshape(KV_RANK, 1), kvg.reshape(1, KV_RANK))


def _dsa_body(qT_ref, iqT_ref, smT_ref, ik_ref, ckv_ref, ckvT_ref, wuk_ref, wuv_ref, o_ref,
              key_ref, qlT_ref, acc_ref, sel_ref, ml_ref, *, topk, tq, idx_bits):
    qi = pl.program_id(1)
    n_kc = qi + 1
    kc_rows = tq

    for j in range(A_HEADS // 2):
        ql = _dot(wuk_ref[j], qT_ref[j * LANE:(j + 1) * LANE, :])
        qlT_ref[j * 2 * KV_RANK:(j + 1) * 2 * KV_RANK, :] = ql.astype(BF16)

    iw = smT_ref[SM_IW:SM_IW + IDX_HEADS, :] * (IDX_HEADS ** -0.5 * IDX_DIM ** -0.5)
    q_chunk = lax.broadcasted_iota(I32, (kc_rows, tq), 1) // CHUNK
    s_chunk = lax.broadcasted_iota(I32, (kc_rows, tq), 0) // CHUNK

    def score_body(kc, carry):
        r0 = pl.multiple_of(kc * kc_rows, kc_rows)
        ikc = ik_ref[pl.ds(r0, kc_rows), :]
        sc = jnp.zeros((kc_rows, tq), F32)
        for h in range(IDX_HEADS):
            rel = jnp.maximum(_dot(ikc, iqT_ref[h * LANE:(h + 1) * LANE, :]), 0.0)
            sc = sc + rel * iw[h:h + 1, :]
        sc = jnp.where(sc == 0.0, 0.0, sc)
        bits = lax.bitcast_convert_type(sc, I32)
        key = jnp.where(bits < 0, bits ^ jnp.int32(0x7FFFFFFF), bits)
        adm = jnp.logical_or(kc < qi, s_chunk <= q_chunk)
        key_ref[pl.ds(r0, kc_rows), :] = jnp.where(adm, key, jnp.int32(INT_MIN))
        return carry

    lax.fori_loop(0, n_kc, score_body, 0)

    def col_count(pred):
        def body(kc, acc):
            r0 = pl.multiple_of(kc * kc_rows, kc_rows)
            m = pred(key_ref[pl.ds(r0, kc_rows), :], r0)
            mi = jnp.where(m, jnp.int32(1), jnp.int32(0))
            return acc + jnp.sum(mi.reshape(kc_rows // SUBLANE, SUBLANE, tq), axis=0)
        acc = lax.fori_loop(0, n_kc, body, jnp.zeros((SUBLANE, tq), I32))
        return jnp.sum(acc, axis=0, keepdims=True)

    sel_ref[0:1, :] = jnp.full((1, tq), INT_MIN, I32)
    sel_ref[1:2, :] = jnp.full((1, tq), -1, I32)

    @pl.when((qi + 1) * tq > topk)
    def _():
        def bit_body(i, thr):
            cand = thr + lax.shift_left(jnp.int32(1), jnp.int32(31) - i)
            cnt = col_count(lambda k, r0: k >= cand)
            return jnp.where(cnt >= topk, cand, thr)
        thr = lax.fori_loop(0, 32, bit_body, jnp.full((1, tq), INT_MIN, I32))
        need = topk - col_count(lambda k, r0: k > thr)

        def idx_body(i, jcut):
            cand = jcut + lax.shift_left(jnp.int32(1), jnp.int32(idx_bits - 1) - i)
            row = lax.broadcasted_iota(I32, (kc_rows, tq), 0)
            f = col_count(lambda k, r0: jnp.logical_and(k == thr, row + r0 < cand))
            return jnp.where(f < need, cand, jcut)
        jcut = lax.fori_loop(0, idx_bits, idx_body, jnp.zeros((1, tq), I32))
        sel_ref[0:1, :] = thr
        sel_ref[1:2, :] = jnp.where(thr == INT_MIN, jnp.int32(-1), jcut)

    thr = sel_ref[0:1, :]
    jcut = sel_ref[1:2, :]

    acc_ref[...] = jnp.zeros_like(acc_ref)
    ml_ref[0:A_HEADS, :] = jnp.full((A_HEADS, tq), MASKED, F32)
    ml_ref[A_HEADS:2 * A_HEADS, :] = jnp.zeros((A_HEADS, tq), F32)

    def attn_body(kc, carry):
        r0 = pl.multiple_of(kc * kc_rows, kc_rows)
        key = key_ref[pl.ds(r0, kc_rows), :]
        row = lax.broadcasted_iota(I32, (kc_rows, tq), 0) + r0
        sel = jnp.logical_or(key > thr, jnp.logical_and(key == thr, row <= jcut))
        bias = jnp.where(sel, 0.0, MASKED)
        ckv_c = ckv_ref[pl.ds(r0, kc_rows), :]
        ckvT_c = ckvT_ref[kc]
        for h in range(A_HEADS):
            lg = _dot(ckv_c, qlT_ref[h * KV_RANK:(h + 1) * KV_RANK, :]) + bias
            m_old = ml_ref[h:h + 1, :]
            m_new = jnp.maximum(m_old, jnp.max(lg, axis=0, keepdims=True))
            alpha = jnp.exp(m_old - m_new)
            p = jnp.exp(lg - m_new)
            ml_ref[h:h + 1, :] = m_new
            ml_ref[A_HEADS + h:A_HEADS + h + 1, :] = (alpha * ml_ref[A_HEADS + h:A_HEADS + h + 1, :]
                                                      + jnp.sum(p, axis=0, keepdims=True))
            rows = slice(h * KV_RANK, (h + 1) * KV_RANK)
            acc_ref[rows, :] = alpha * acc_ref[rows, :] + _dot(ckvT_c, p.astype(BF16))
        return carry

    lax.fori_loop(0, n_kc, attn_body, 0)

    for j in range(A_HEADS // 2):
        parts = []
        for h in (2 * j, 2 * j + 1):
            o_latT = acc_ref[h * KV_RANK:(h + 1) * KV_RANK, :] / ml_ref[A_HEADS + h:A_HEADS + h + 1, :]
            parts.append(o_latT.T)
        o_lat = jnp.concatenate(parts, axis=1).astype(BF16)
        o_ref[:, j * LANE:(j + 1) * LANE] = _dot(o_lat, wuv_ref[j]).astype(BF16)


def _dsa(qT, iqT, smT, ik, ckv, ckvT, wuk_bd, wuv_bd, tq=128):
    b, _, t = qT.shape
    topk = min(TOPK_MAX, t // 4)
    nq = t // tq
    colq = lambda r: pl.BlockSpec((None, r, tq), lambda bi, i: (bi, 0, i))
    seq = lambda w: pl.BlockSpec((None, t, w), lambda bi, i: (bi, 0, 0))
    return pl.pallas_call(
        functools.partial(_dsa_body, topk=topk, tq=tq, idx_bits=int(t).bit_length()),
        grid=(b, nq),
        in_specs=[colq(512), colq(512), colq(16), seq(LANE), seq(KV_RANK),
                  pl.BlockSpec((None, t // LANE, KV_RANK, LANE), lambda bi, i: (bi, 0, 0, 0)),
                  _const_spec(wuk_bd.shape), _const_spec(wuv_bd.shape)],
        out_specs=pl.BlockSpec((None, tq, A_WIDTH), lambda bi, i: (bi, i, 0)),
        out_shape=jax.ShapeDtypeStruct((b, t, A_WIDTH), BF16),
        scratch_shapes=[pltpu.VMEM((t, tq), I32),
                        pltpu.VMEM((A_HEADS * KV_RANK, tq), BF16),
                        pltpu.VMEM((A_HEADS * KV_RANK, tq), F32),
                        pltpu.VMEM((SUBLANE, tq), I32),
                        pltpu.VMEM((2 * A_HEADS, tq), F32)],
        compiler_params=_params("parallel", "arbitrary"), name="dsa",
    )(qT, iqT, smT, ik, ckv, ckvT, wuk_bd, wuv_bd)


def _s5_disc_body(are_ref, aim_ref, ldt_ref, bre_ref, bim_ref, abre_ref, abim_ref, bbre_ref, bbim_ref):
    a_re, a_im = are_ref[...], aim_ref[...]
    dt = jnp.exp(ldt_ref[...])
    mag = jnp.exp(dt * a_re)
    ang = dt * a_im
    ab_re = mag * jnp.cos(ang)
    ab_im = mag * jnp.sin(ang)
    den = a_re * a_re + a_im * a_im
    n_re = ab_re - 1.0
    f_re = (n_re * a_re + ab_im * a_im) / den
    f_im = (ab_im * a_re - n_re * a_im) / den
    abre_ref[...] = ab_re
    abim_ref[...] = ab_im
    b_re, b_im = bre_ref[...], bim_ref[...]
    bbre_ref[...] = f_re[:, None, :] * b_re - f_im[:, None, :] * b_im
    bbim_ref[...] = f_re[:, None, :] * b_im + f_im[:, None, :] * b_re


def _s5_discretise(a_re, a_im, log_dt, bT_re, bT_im):
    g, p = a_re.shape
    m = bT_re.shape[1]
    return pl.pallas_call(
        _s5_disc_body,
        out_shape=(jax.ShapeDtypeStruct((g, p), F32), jax.ShapeDtypeStruct((g, p), F32),
                   jax.ShapeDtypeStruct((g, m, p), F32), jax.ShapeDtypeStruct((g, m, p), F32)),
        name="s5_disc",
    )(a_re, a_im, log_dt.reshape(g, 1), bT_re, bT_im)


S5_BLK = 8
S5_NBLK = S5_GROUPS // S5_BLK
S5_HALF = S5_BLK * S5_STATE


def _s5_body(u_ref, bblk_ref, are_ref, aim_ref, cblk_ref, d_ref, wglu_ref, o_ref,
             hst_ref, xbuf_ref, y_ref, *, tt, nb):
    @pl.when(pl.program_id(0) == 0)
    def _():
        hst_ref[...] = jnp.zeros_like(hst_ref)

    for j in range(S5_NBLK):
        xbuf_ref[...] = _dot(u_ref[:, j * LANE:(j + 1) * LANE], bblk_ref[j])
        a_r = jnp.broadcast_to(are_ref[j], (nb, S5_HALF))
        a_i = jnp.broadcast_to(aim_ref[j], (nb, S5_HALF))

        def step(ti, carry):
            h_r, h_i = carry
            r0 = pl.multiple_of(ti * nb, nb)
            n_r = a_r * h_r - a_i * h_i + xbuf_ref[pl.ds(r0, nb), 0:S5_HALF]
            n_i = a_r * h_i + a_i * h_r + xbuf_ref[pl.ds(r0, nb), S5_HALF:2 * S5_HALF]
            xbuf_ref[pl.ds(r0, nb), 0:S5_HALF] = n_r
            xbuf_ref[pl.ds(r0, nb), S5_HALF:2 * S5_HALF] = n_i
            return n_r, n_i

        h_r, h_i = lax.fori_loop(0, tt, step, (hst_ref[j, 0], hst_ref[j, 1]), unroll=8)
        hst_ref[j, 0] = h_r
        hst_ref[j, 1] = h_i
        y_ref[:, j * LANE:(j + 1) * LANE] = _dot(xbuf_ref[...].astype(BF16), cblk_ref[j])

    y = y_ref[...] + d_ref[...] * u_ref[...].astype(F32)
    y = jax.nn.gelu(y)
    o_ref[...] = (y * jax.nn.sigmoid(_dot(y.astype(BF16), wglu_ref[...]))).astype(BF16)


def _s5(u_tm, bblk, a_re, a_im, cblk, d, wglu, nb, tt=128):
    rows, w = u_tm.shape
    t = rows // nb
    blk = pl.BlockSpec((tt * nb, w), lambda i: (i, 0))
    return pl.pallas_call(
        functools.partial(_s5_body, tt=tt, nb=nb),
        grid=(t // tt,),
        in_specs=[blk, _const_spec(bblk.shape), _const_spec(a_re.shape), _const_spec(a_im.shape),
                  _const_spec(cblk.shape), _const_spec((1, w)), _const_spec(wglu.shape)],
        out_specs=blk, out_shape=jax.ShapeDtypeStruct((rows, w), BF16),
        scratch_shapes=[pltpu.VMEM((S5_NBLK, 2, nb, S5_HALF), F32),
                        pltpu.VMEM((tt * nb, 2 * S5_HALF), F32),
                        pltpu.VMEM((tt * nb, w), F32)],
        compiler_params=_params("arbitrary"), name="s5",
    )(u_tm, bblk, a_re, a_im, cblk, d.reshape(1, w), wglu)


GDN_ROWS = 128
GDN_HALO = SUBLANE


def _gdn_body(qkv_ref, z_ref, ab_ref, aT_ref, cw_ref, alog_row_ref, dtb_row_ref, alog_col_ref,
              dtb_col_ref, ng_ref, o_ref, tail_ref, xpad_ref, s_ref):
    n = GDN_ROWS

    @pl.when(pl.program_id(1) == 0)
    def _():
        tail_ref[...] = jnp.zeros_like(tail_ref)
        s_ref[...] = jnp.zeros_like(s_ref)

    xpad_ref[0:GDN_HALO, :] = tail_ref[...]
    xpad_ref[GDN_HALO:GDN_HALO + n, :] = qkv_ref[...].astype(F32)
    tail_ref[...] = xpad_ref[n:n + GDN_HALO, :]
    x = cw_ref[0:1, :] * xpad_ref[pl.ds(GDN_HALO - C_CONV + 1, n), :]
    for j in range(1, C_CONV):
        x = x + cw_ref[j:j + 1, :] * xpad_ref[pl.ds(GDN_HALO - C_CONV + 1 + j, n), :]
    x = _silu(x)

    def softplus(v):
        return jnp.maximum(v, 0.0) + jnp.log1p(jnp.exp(-jnp.abs(v)))

    ab = ab_ref[...]
    g_cols = -jnp.exp(alog_row_ref[...]) * softplus(ab + dtb_row_ref[...])
    beta_cols = jax.nn.sigmoid(ab)
    g_rows = -jnp.exp(alog_col_ref[...]) * softplus(aT_ref[...] + dtb_col_ref[...])
    ri = lax.broadcasted_iota(I32, (n, n), 0)
    ci = lax.broadcasted_iota(I32, (n, n), 1)
    lower = ri >= ci
    strict = ri > ci
    same = (ri // CHUNK) == (ci // CHUNK)
    gam_cols = _dot_f32(jnp.where(lower, 1.0, 0.0).astype(F32), g_cols)
    gam_rows = _dot_f32(g_rows, jnp.where(ri <= ci, 1.0, 0.0).astype(F32))
    eye = jnp.where(ri == ci, 1.0, 0.0).astype(F32)

    for h in range(C_HEADS):
        sl = slice(h * C_HEAD_DIM, (h + 1) * C_HEAD_DIM)
        q = x[:, sl]
        k = x[:, C_WIDTH + h * C_HEAD_DIM:C_WIDTH + (h + 1) * C_HEAD_DIM]
        v = x[:, 2 * C_WIDTH + h * C_HEAD_DIM:2 * C_WIDTH + (h + 1) * C_HEAD_DIM]
        q = q * lax.rsqrt(jnp.sum(q * q, axis=-1, keepdims=True) + EPS) * (C_HEAD_DIM ** -0.5)
        k = k * lax.rsqrt(jnp.sum(k * k, axis=-1, keepdims=True) + EPS)
        gc = gam_cols[:, h:h + 1]
        gr = gam_rows[SM_GA + h:SM_GA + h + 1, :]
        beta = beta_cols[:, C_HEADS + h:C_HEADS + h + 1]
        decay = jnp.exp(jnp.where(lower, gc - gr, MASKED))
        kb = k * beta
        kbb, kbf = kb.astype(BF16), k.astype(BF16)
        m_full = jnp.where(strict, _dot_nt(kbb, kbf) * decay, 0.0)
        m_bd = jnp.where(same, m_full, 0.0)
        m_off = m_full - m_bd
        mb = m_bd.astype(BF16)
        xinv = eye - m_bd
        pw = _dot(mb, mb)
        for lvl in range(5):
            pwb = pw.astype(BF16)
            xinv = xinv + _dot(xinv.astype(BF16), pwb)
            if lvl < 4:
                pw = _dot(pwb, pwb)
        xb = xinv.astype(BF16)
        xm = _dot(xb, m_off.astype(BF16))
        xinv = xinv - _dot(xm.astype(BF16), xb)
        eg = jnp.exp(gc)
        rhs = jnp.concatenate([v * beta, kb * eg], axis=1).astype(BF16)
        sol = _dot(xinv.astype(BF16), rhs)
        u_c, w_c = sol[:, :C_HEAD_DIM], sol[:, C_HEAD_DIM:]
        attn = _dot_nt(q.astype(BF16), kbf) * decay
        g_last = gc[n - 1:n, :]
        q_dec = q * eg
        k_dec = k * jnp.exp(g_last - gc)
        s_old = s_ref[h]
        sb = s_old.astype(BF16)
        v_new = u_c - _dot(w_c.astype(BF16), sb)
        vb = v_new.astype(BF16)
        o = _dot(q_dec.astype(BF16), sb) + _dot(attn.astype(BF16), vb)
        s_ref[h] = s_old * jnp.exp(g_last) + _dot(k_dec.T.astype(BF16), vb)
        o = _rms(o, ng_ref[...]) * _silu(z_ref[:, sl].astype(F32))
        o_ref[:, sl] = o.astype(BF16)


def _gdn(qkv, z, ab, smT, conv_w, a_log, dt_bias, norm_g):
    b, t, _ = qkv.shape
    n = GDN_ROWS
    row = lambda w: pl.BlockSpec((None, n, w), lambda bi, i: (bi, i, 0))
    pad_row = lambda v: jnp.zeros((1, LANE), F32).at[0, :C_HEADS].set(v)
    pad_col = lambda v: jnp.zeros((16, 1), F32).at[SM_GA:SM_GA + C_HEADS, 0].set(v)
    return pl.pallas_call(
        _gdn_body,
        grid=(b, t // n),
        in_specs=[row(3 * C_WIDTH), row(C_WIDTH), row(LANE),
                  pl.BlockSpec((None, 16, n), lambda bi, i: (bi, 0, i)),
                  _const_spec((C_CONV, 3 * C_WIDTH)), _const_spec((1, LANE)), _const_spec((1, LANE)),
                  _const_spec((16, 1)), _const_spec((16, 1)), _const_spec((1, C_HEAD_DIM))],
        out_specs=row(C_WIDTH), out_shape=jax.ShapeDtypeStruct((b, t, C_WIDTH), BF16),
        scratch_shapes=[pltpu.VMEM((GDN_HALO, 3 * C_WIDTH), F32),
                        pltpu.VMEM((GDN_HALO + n, 3 * C_WIDTH), F32),
                        pltpu.VMEM((C_HEADS, C_HEAD_DIM, C_HEAD_DIM), F32)],
        compiler_params=_params("parallel", "arbitrary"), name="gdn",
    )(qkv, z, ab, smT, conv_w, pad_row(a_log), pad_row(dt_bias), pad_col(a_log), pad_col(dt_bias),
      norm_g.reshape(1, C_HEAD_DIM))


def _merge_body(x_ref, oa_ref, yb_ref, oc_ref, gate_ref, wpa_ref, wpb_ref, wpc_ref, wo_ref, o_ref):
    d = D_MODEL
    mix = jax.nn.sigmoid(gate_ref[:, 0:d].astype(F32)) * _dot(oa_ref[...], wpa_ref[...])
    mix = mix + jax.nn.sigmoid(gate_ref[:, d:2 * d].astype(F32)) * _dot(yb_ref[...], wpb_ref[...])
    mix = mix + jax.nn.sigmoid(gate_ref[:, 2 * d:3 * d].astype(F32)) * _dot(oc_ref[...], wpc_ref[...])
    o_ref[...] = x_ref[...] + _dot(mix.astype(BF16), wo_ref[...])


def _merge(x, oa, yb, oc, gates, wpa, wpb, wpc, wo, tm=512):
    b, t, d = x.shape
    row = lambda w: pl.BlockSpec((None, tm, w), lambda bi, i: (bi, i, 0))
    return pl.pallas_call(
        _merge_body,
        grid=(b, t // tm),
        in_specs=[row(d), row(A_WIDTH), row(S5_WIDTH), row(C_WIDTH), row(N_BRANCH * d),
                  _const_spec(wpa.shape), _const_spec(wpb.shape), _const_spec(wpc.shape),
                  _const_spec(wo.shape)],
        out_specs=row(d), out_shape=jax.ShapeDtypeStruct((b, t, d), F32),
        compiler_params=_params("parallel", "parallel"), name="merge",
    )(x, oa, yb, oc, gates, wpa, wpb, wpc, wo)


FFN_TILE = 256
FFN_HALO = 2 * SUBLANE


def _ffn_body(x_ref, xprev_ref, g_ref, wup_ref, cw_ref, wdn_ref, fg_ref, o_ref, hcat_ref, hid_ref, acc_ref,
              *, tm, final_norm):
    first = pl.program_id(1) == 0
    hcat_ref[0:FFN_HALO, :] = _rms(xprev_ref[...], g_ref[...]).astype(BF16)
    hcat_ref[FFN_HALO:FFN_HALO + tm, :] = _rms(x_ref[...], g_ref[...]).astype(BF16)
    hb = hcat_ref[...]
    keep = jnp.where(jnp.logical_and(
        first, lax.broadcasted_iota(I32, (FFN_HALO + tm, 1), 0) < FFN_HALO), 0.0, 1.0)

    def conv(cols):
        hid_ref[...] = _dot(hb, wup_ref[:, cols]) * keep
        base = FFN_HALO - FFN_CONV + 1
        out = cw_ref[0:1, cols] * hid_ref[pl.ds(base, tm), :]
        for j in range(1, FFN_CONV):
            out = out + cw_ref[j:j + 1, cols] * hid_ref[pl.ds(base + j, tm), :]
        return out

    acc_ref[...] = jnp.zeros_like(acc_ref)
    for j in range(D_FF // FFN_TILE):
        gate = conv(slice(j * FFN_TILE, (j + 1) * FFN_TILE))
        val = conv(slice(D_FF + j * FFN_TILE, D_FF + (j + 1) * FFN_TILE))
        act = (_silu(gate) * val).astype(BF16)
        acc_ref[...] += _dot(act, wdn_ref[j * FFN_TILE:(j + 1) * FFN_TILE, :])
    y = x_ref[...] + acc_ref[...]
    if final_norm:
        y = _rms(y, fg_ref[...])
    o_ref[...] = y


def _ffn(x, g, wup, cw, wdn, final_g, final_norm, tm=512):
    b, t, d = x.shape
    row = pl.BlockSpec((None, tm, d), lambda bi, i: (bi, i, 0))
    halo_blocks = tm // FFN_HALO
    prev = pl.BlockSpec((None, FFN_HALO, d), lambda bi, i: (bi, jnp.maximum(i * halo_blocks - 1, 0), 0))
    return pl.pallas_call(
        functools.partial(_ffn_body, tm=tm, final_norm=final_norm),
        grid=(b, t // tm),
        in_specs=[row, prev, _const_spec((1, d)), _const_spec(wup.shape), _const_spec(cw.shape),
                  _const_spec(wdn.shape), _const_spec((1, d))],
        out_specs=row, out_shape=jax.ShapeDtypeStruct((b, t, d), F32),
        scratch_shapes=[pltpu.VMEM((FFN_HALO + tm, d), BF16),
                        pltpu.VMEM((FFN_HALO + tm, FFN_TILE), F32),
                        pltpu.VMEM((tm, d), F32)],
        compiler_params=_params("parallel", "parallel"), name="ffn",
    )(x, x, g.reshape(1, d), wup, cw, wdn, final_g.reshape(1, d))


def _pair_block_diag(w):
    h, r, c = w.shape
    w = w.reshape(h // 2, 2, r, c)
    z = jnp.zeros((h // 2, r, c), w.dtype)
    top = jnp.concatenate([w[:, 0], z], axis=2)
    bot = jnp.concatenate([z, w[:, 1]], axis=2)
    return jnp.concatenate([top, bot], axis=1)


def _pack_in_proj(w_in):
    o = np.cumsum((0,) + IN_SPLITS)
    part = lambda i: w_in[:, o[i]:o[i + 1]]
    w_q, w_ckv, w_iq, w_ik, w_iw, w_s5, w_qkv, w_z, w_ca, w_cb, w_gate = (part(i) for i in range(11))
    d = w_in.shape[0]
    iq_t = w_iq.T.reshape(IDX_HEADS, IDX_DIM, d)
    iq_t = jnp.pad(iq_t, ((0, 0), (0, LANE - IDX_DIM), (0, 0))).reshape(IDX_HEADS * LANE, d)
    small = jnp.zeros((16, d), w_in.dtype)
    small = small.at[SM_IW:SM_IW + IDX_HEADS].set(w_iw.T).at[SM_GA:SM_GA + C_HEADS].set(w_ca.T)
    wt = jnp.concatenate([w_q.T, iq_t, w_ckv.T, small], axis=0).astype(BF16)
    ab = jnp.pad(jnp.concatenate([w_ca, w_cb], axis=1), ((0, 0), (0, LANE - 2 * C_HEADS)))
    wn = jnp.concatenate([w_ckv, jnp.pad(w_ik, ((0, 0), (0, LANE - IDX_DIM))), w_s5, w_qkv, w_z, ab,
                          w_gate], axis=1).astype(BF16)
    assert wt.shape[0] == WT_ROWS and wn.shape[1] == WN_COLS
    return wt, wn


def _pack_s5(bb_re, bb_im, ab_re, ab_im, c_re, c_im):
    eye = jnp.eye(S5_BLK, dtype=F32)
    def in_map(bbT):
        w = bbT.reshape(S5_NBLK, S5_BLK, S5_GROUP, S5_STATE)
        return jnp.einsum('jgmp,gh->jgmhp', w, eye).reshape(S5_NBLK, S5_BLK * S5_GROUP, S5_HALF)
    def out_map(c):
        w = c.reshape(S5_NBLK, S5_BLK, S5_GROUP, S5_STATE)
        return jnp.einsum('jgmp,gh->jgphm', w, eye).reshape(S5_NBLK, S5_HALF, S5_BLK * S5_GROUP)
    bblk = jnp.concatenate([in_map(bb_re), in_map(bb_im)], axis=2).astype(BF16)
    cblk = jnp.concatenate([out_map(c_re), out_map(-c_im)], axis=1).astype(BF16)
    return bblk, ab_re.reshape(S5_NBLK, 1, S5_HALF), ab_im.reshape(S5_NBLK, 1, S5_HALF), cblk


def kernel(x, attn_norm_g, w_in, kv_norm_g, w_uk, w_uv, w_proj_a, s5_a_re, s5_a_im, s5_log_dt, s5_b_re,
           s5_b_im, s5_c_re, s5_c_im, s5_d, w_glu, w_proj_b, gdn_conv_w, gdn_a_log, gdn_dt_bias,
           gdn_norm_g, w_proj_c, w_out, ffn_norm_g, w_up, ffn_conv_w, w_down, final_norm_g):
    b, t, d = x.shape
    depth = w_in.shape[0]
    for l in range(depth):
        wt, wn = _pack_in_proj(w_in[l])
        (qT, iqT, ckvT, smT, ckv, ik, u, qkv, z, ab, gates) = _in_proj(x, attn_norm_g[l], wt, wn, kv_norm_g[l])

        wuk_bd = _pair_block_diag(jnp.transpose(w_uk[l], (1, 0, 2)) * (A_HEAD_DIM ** -0.5)).astype(BF16)
        wuv_bd = _pair_block_diag(jnp.transpose(w_uv[l], (1, 0, 2))).astype(BF16)
        oa = _dsa(qT, iqT, smT, ik, ckv, ckvT, wuk_bd, wuv_bd)

        ab_re, ab_im, bb_re, bb_im = _s5_discretise(
            s5_a_re[l], s5_a_im[l], s5_log_dt[l],
            jnp.transpose(s5_b_re[l], (0, 2, 1)), jnp.transpose(s5_b_im[l], (0, 2, 1)))
        bblk, a_r, a_i, cblk = _pack_s5(bb_re, bb_im, ab_re, ab_im, s5_c_re[l], s5_c_im[l])
        u_tm = jnp.transpose(u, (1, 0, 2)).reshape(t * b, S5_WIDTH)
        yb_tm = _s5(u_tm, bblk, a_r, a_i, cblk, s5_d[l], w_glu[l].astype(BF16), nb=b)
        yb = jnp.transpose(yb_tm.reshape(t, b, S5_WIDTH), (1, 0, 2))

        oc = _gdn(qkv, z, ab, smT, gdn_conv_w[l], gdn_a_log[l], gdn_dt_bias[l], gdn_norm_g[l])

        x = _merge(x, oa, yb, oc, gates, w_proj_a[l].astype(BF16), w_proj_b[l].astype(BF16),
                   w_proj_c[l].astype(BF16), w_out[l].astype(BF16))
        x = _ffn(x, ffn_norm_g[l], w_up[l].astype(BF16), ffn_conv_w[l], w_down[l].astype(BF16),
                 final_norm_g, final_norm=(l == depth - 1))
    return x
```

```python
import functools
import math

import numpy as np
import jax
import jax.numpy as jnp
from jax import lax
from jax.experimental import pallas as pl
from jax.experimental.pallas import tpu as pltpu

F32 = jnp.float32
BF16 = jnp.bfloat16
I32 = jnp.int32
I16 = jnp.int16

D_MODEL = 1024
EPS = 1e-6
CHUNK = 64
A_HEADS, A_HEAD_DIM, KV_RANK = 8, 64, 128
A_WIDTH = A_HEADS * A_HEAD_DIM
IDX_HEADS, IDX_DIM = 4, 64
TOPK_MAX = 256
S5_WIDTH, S5_GROUP, S5_STATE = 512, 16, 64
S5_GROUPS = S5_WIDTH // S5_GROUP
C_HEADS, C_HEAD_DIM, C_CONV = 4, 128, 4
C_WIDTH = C_HEADS * C_HEAD_DIM
D_FF, FFN_CONV = 2816, 3
N_BRANCH = 3
IN_SPLITS = (A_WIDTH, KV_RANK, IDX_HEADS * IDX_DIM, IDX_DIM, IDX_HEADS, S5_WIDTH,
             3 * C_WIDTH, C_WIDTH, C_HEADS, C_HEADS, N_BRANCH * D_MODEL)

LANE = 128
SUBLANE = 8
VMEM_LIMIT = 56 * 1024 * 1024
MASKED = -1e30
INT_MIN = -2 ** 31
DSA_KC = 512
NO_CUT = 2 ** 30
KV_EXT = KV_RANK + 16
HALF16 = 2 ** 15

WT_Q, WT_IQ, WT_CKV, WT_SMALL = 0, 512, 1024, 1152
WT_ROWS = 1168
SM_IW, SM_GA, SM_GB = 0, 8, 12
WN_CKV, WN_IK, WN_S5, WN_QKV, WN_Z, WN_GATE, WN_COLS = 0, 128, 256, 768, 2304, 2816, 5888


def _dot(a, b):
    return jnp.dot(a, b, preferred_element_type=F32)


def _dot_nt(a, b):
    return lax.dot_general(a, b, (((1,), (1,)), ((), ())), preferred_element_type=F32)


def _dot_f32(a, b):
    return jnp.dot(a, b, preferred_element_type=F32, precision=lax.Precision.HIGHEST)


def _rms(x, g):
    return x * lax.rsqrt(jnp.mean(x * x, axis=-1, keepdims=True) + EPS) * g


def _silu(x):
    return x * jax.nn.sigmoid(x)


def _const_spec(shape):
    nd = len(shape)
    return pl.BlockSpec(shape, lambda *_: (0,) * nd, pipeline_mode=pl.Buffered(1))


def _params(*sem):
    return pltpu.CompilerParams(dimension_semantics=sem, vmem_limit_bytes=VMEM_LIMIT)


def _in_proj_body(x_ref, g_ref, wt_ref, wn_ref, kvg_col_ref, kvg_row_ref,
                  qT_ref, iqT_ref, ckvT_ref, smT_ref, ckv_ref, ik_ref, u_ref, qkv_ref, z_ref,
                  gate_ref, *, tm):
    hb = _rms(x_ref[...], g_ref[...]).astype(BF16)
    qT_ref[...] = _dot_nt(wt_ref[WT_Q:WT_IQ, :], hb).astype(BF16)
    iqT_ref[...] = _dot_nt(wt_ref[WT_IQ:WT_CKV, :], hb).astype(BF16)
    cT = _dot_nt(wt_ref[WT_CKV:WT_SMALL, :], hb)
    cT = cT * lax.rsqrt(jnp.mean(cT * cT, axis=0, keepdims=True) + EPS) * kvg_col_ref[...]
    for j in range(tm // DSA_KC):
        ckvT_ref[j, 0:KV_RANK, :] = cT[:, j * DSA_KC:(j + 1) * DSA_KC].astype(BF16)
        ckvT_ref[j, KV_RANK:KV_EXT, :] = jnp.ones((KV_EXT - KV_RANK, DSA_KC), BF16)
    smT_ref[...] = _dot_nt(wt_ref[WT_SMALL:WT_ROWS, :], hb)
    c = _dot(hb, wn_ref[:, WN_CKV:WN_IK])
    ckv_ref[...] = _rms(c, kvg_row_ref[...]).astype(BF16)
    ik_ref[...] = _dot(hb, wn_ref[:, WN_IK:WN_S5]).astype(BF16)
    u_ref[...] = _dot(hb, wn_ref[:, WN_S5:WN_QKV]).astype(BF16)
    for j in range(3):
        lo = WN_QKV + j * C_WIDTH
        qkv_ref[:, j * C_WIDTH:(j + 1) * C_WIDTH] = _dot(hb, wn_ref[:, lo:lo + C_WIDTH]).astype(BF16)
    z_ref[...] = _dot(hb, wn_ref[:, WN_Z:WN_GATE]).astype(BF16)
    for j in range(2 * N_BRANCH):
        lo = WN_GATE + j * 512
        gate_ref[:, j * 512:(j + 1) * 512] = _dot(hb, wn_ref[:, lo:lo + 512]).astype(BF16)


def _in_proj(x, g, wt, wn, kvg, tm=DSA_KC):
    b, t, d = x.shape
    nt = t // tm
    row = lambda w: pl.BlockSpec((None, tm, w), lambda bi, i: (bi, i, 0))
    col = lambda r: pl.BlockSpec((None, r, tm), lambda bi, i: (bi, 0, i))
    out_shape = (
        jax.ShapeDtypeStruct((b, 512, t), BF16),
        jax.ShapeDtypeStruct((b, 512, t), BF16),
        jax.ShapeDtypeStruct((b, t // DSA_KC, KV_EXT, DSA_KC), BF16),
        jax.ShapeDtypeStruct((b, 16, t), F32),
        jax.ShapeDtypeStruct((b, t, KV_RANK), BF16),
        jax.ShapeDtypeStruct((b, t, LANE), BF16),
        jax.ShapeDtypeStruct((t, b * S5_WIDTH), BF16),
        jax.ShapeDtypeStruct((b, t, 3 * C_WIDTH), BF16),
        jax.ShapeDtypeStruct((b, t, C_WIDTH), BF16),
        jax.ShapeDtypeStruct((b, t, N_BRANCH * D_MODEL), BF16),
    )
    out_specs = (
        col(512), col(512),
        pl.BlockSpec((None, tm // DSA_KC, KV_EXT, DSA_KC), lambda bi, i: (bi, i, 0, 0)),
        col(16), row(KV_RANK), row(LANE),
        pl.BlockSpec((tm, S5_WIDTH), lambda bi, i: (i, bi)),
        row(3 * C_WIDTH), row(C_WIDTH), row(N_BRANCH * D_MODEL),
    )
    assert tm % DSA_KC == 0
    return pl.pallas_call(
        functools.partial(_in_proj_body, tm=tm),
        grid=(b, nt),
        in_specs=[row(d), _const_spec((1, d)), _const_spec(wt.shape), _const_spec(wn.shape),
                  _const_spec((KV_RANK, 1)), _const_spec((1, KV_RANK))],
        out_specs=out_specs, out_shape=out_shape,
        compiler_params=_params("parallel", "parallel"), name="in_proj",
    )(x, g.reshape(1, d), wt, wn, kvg.reshape(KV_RANK, 1), kvg.reshape(1, KV_RANK))


def _dsa_body(qT_ref, iqT_ref, smT_ref, ik_ref, ckv_ref, ckvT_ref, wuk_ref, wuv_ref, o_ref,
              key_ref, hi_ref, lo_ref, qlT_ref, lg_ref, acc_ref, sel_ref, *, topk, tq, idx_bits):
    kc_rows = DSA_KC
    hq = A_HEADS * tq
    qi = pl.program_id(1)
    n_kc = (qi * tq + tq + kc_rows - 1) // kc_rows

    def rows_of(kc):
        return pl.ds(pl.multiple_of(kc * kc_rows, kc_rows), kc_rows)

    for j in range(A_HEADS // 2):
        ql = _dot(wuk_ref[j], qT_ref[j * LANE:(j + 1) * LANE, :]).astype(BF16)
        qlT_ref[:, (2 * j) * tq:(2 * j + 1) * tq] = ql[0:KV_RANK]
        qlT_ref[:, (2 * j + 1) * tq:(2 * j + 2) * tq] = ql[KV_RANK:2 * KV_RANK]

    iw = smT_ref[SM_IW:SM_IW + IDX_HEADS, :] * (IDX_HEADS ** -0.5 * IDX_DIM ** -0.5)
    chunk_shift = CHUNK.bit_length() - 1
    q_chunk = lax.shift_right_logical(qi * tq + lax.broadcasted_iota(I32, (kc_rows, tq), 1), chunk_shift)
    row_iota = lax.broadcasted_iota(I32, (kc_rows, tq), 0)

    def score_body(kc, carry):
        ikc = ik_ref[rows_of(kc), :]
        sc = jnp.zeros((kc_rows, tq), F32)
        for h in range(IDX_HEADS):
            rel = jnp.maximum(_dot(ikc, iqT_ref[h * LANE:(h + 1) * LANE, :]), 0.0)
            sc = sc + rel * iw[h:h + 1, :]
        bits = lax.bitcast_convert_type(sc, I32)
        mag = bits & jnp.int32(0x7FFFFFFF)
        key = jnp.where(bits < 0, -mag, mag)
        adm = lax.shift_right_logical(row_iota + kc * kc_rows, chunk_shift) <= q_chunk
        key = jnp.where(adm, key, jnp.int32(INT_MIN))
        key_ref[rows_of(kc), :] = key
        hi_ref[rows_of(kc), :] = lax.shift_right_arithmetic(key, 16).astype(I16)
        lo_ref[rows_of(kc), :] = ((key & jnp.int32(0xFFFF)) - jnp.int32(HALF16)).astype(I16)
        return carry

    lax.fori_loop(0, n_kc, score_body, 0)

    def col_count(pred):
        lanes = 4 * SUBLANE
        def body(kc, acc):
            m = pred(key_ref[rows_of(kc), :], kc * kc_rows)
            mi = jnp.where(m, jnp.int32(1), jnp.int32(0))
            return acc + jnp.sum(mi.reshape(kc_rows // lanes, lanes, tq), axis=0)
        acc = lax.fori_loop(0, n_kc, body, jnp.zeros((lanes, tq), I32))
        return jnp.sum(acc, axis=0, keepdims=True)

    def col_count16(src_ref, pred):
        rows = 4 * SUBLANE
        def body(kc, acc):
            ones = jnp.where(pred(src_ref[rows_of(kc), :]), jnp.ones((), BF16), jnp.zeros((), BF16))
            parts = [ones[r * rows:(r + 1) * rows] for r in range(kc_rows // rows)]
            while len(parts) > 1:
                parts = [parts[i] + parts[i + 1] for i in range(0, len(parts), 2)]
            return acc + parts[0].astype(F32)
        acc = lax.fori_loop(0, n_kc, body, jnp.zeros((rows, tq), F32))
        return jnp.sum(acc, axis=0, keepdims=True).astype(I32)

    def kth_largest16(src_ref, rank):
        def bit_body(i, thr):
            cand = thr + lax.shift_left(jnp.int32(1), jnp.int32(15) - i)
            cand16 = cand.astype(I16)
            cnt = col_count16(src_ref, lambda v: v >= cand16)
            return jnp.where(cnt >= rank, cand, thr)
        return lax.fori_loop(0, 16, bit_body, jnp.full((1, tq), -HALF16, I32))

    sel_ref[0:1, :] = jnp.full((1, tq), INT_MIN, I32)
    sel_ref[1:2, :] = jnp.full((1, tq), -1, I32)

    @pl.when((qi + 1) * tq > topk)
    def _():
        thr_hi = kth_largest16(hi_ref, topk)
        thr_hi16 = thr_hi.astype(I16)
        rank_lo = topk - col_count16(hi_ref, lambda v: v > thr_hi16)

        def bucket_body(kc, carry):
            in_bucket = hi_ref[rows_of(kc), :] == thr_hi16
            lo_ref[rows_of(kc), :] = jnp.where(in_bucket, lo_ref[rows_of(kc), :], jnp.int16(-HALF16))
            return carry
        lax.fori_loop(0, n_kc, bucket_body, 0)
        thr_lo = kth_largest16(lo_ref, rank_lo)
        thr = lax.shift_left(thr_hi, 16) + (thr_lo + jnp.int32(HALF16))
        open_thr = thr != INT_MIN
        need = topk - col_count(lambda k, r0: k > thr)
        surplus = jnp.logical_and(col_count(lambda k, r0: k == thr) > need, open_thr)
        sel_ref[0:1, :] = thr
        sel_ref[1:2, :] = jnp.where(open_thr, jnp.int32(NO_CUT), jnp.int32(-1))

        @pl.when(jnp.max(jnp.where(surplus, 1, 0)) > 0)
        def _():
            def idx_body(i, jcut):
                cand = jcut + lax.shift_left(jnp.int32(1), jnp.int32(idx_bits - 1) - i)
                f = col_count(lambda k, r0: jnp.logical_and(k == thr, row_iota + r0 < cand))
                return jnp.where(f < need, cand, jcut)
            jcut = lax.fori_loop(0, idx_bits, idx_body, jnp.zeros((1, tq), I32))
            sel_ref[1:2, :] = jnp.where(open_thr, jcut, jnp.int32(-1))

    thr = sel_ref[0:1, :]
    jcut = sel_ref[1:2, :]

    def logit_body(kc, m8):
        key = key_ref[rows_of(kc), :]
        sel = jnp.logical_or(key > thr, jnp.logical_and(key == thr, row_iota + kc * kc_rows <= jcut))
        bias = jnp.where(sel, 0.0, MASKED)
        lg = _dot(ckv_ref[rows_of(kc), :], qlT_ref[...])
        parts = []
        for h in range(A_HEADS):
            x = lg[:, h * tq:(h + 1) * tq] + bias
            lg_ref[rows_of(kc), h * tq:(h + 1) * tq] = x
            parts.append(jnp.max(x.reshape(kc_rows // SUBLANE, SUBLANE, tq), axis=0))
        return jnp.maximum(m8, jnp.concatenate(parts, axis=1))

    m8 = lax.fori_loop(0, n_kc, logit_body, jnp.full((SUBLANE, hq), MASKED, F32))
    m = jnp.max(m8, axis=0, keepdims=True)

    acc_ref[...] = jnp.zeros_like(acc_ref)

    def pv_body(kc, carry):
        p = jnp.exp2(lg_ref[rows_of(kc), :] - m)
        acc_ref[...] += _dot(ckvT_ref[kc], p.astype(BF16))
        return carry

    lax.fori_loop(0, n_kc, pv_body, 0)
    l = acc_ref[KV_RANK:KV_RANK + 1, :]

    for j in range(A_HEADS // 2):
        parts = []
        for h in (2 * j, 2 * j + 1):
            o_latT = acc_ref[0:KV_RANK, h * tq:(h + 1) * tq] / l[:, h * tq:(h + 1) * tq]
            parts.append(o_latT.T)
        o_lat = jnp.concatenate(parts, axis=1).astype(BF16)
        o_ref[:, j * LANE:(j + 1) * LANE] = _dot(o_lat, wuv_ref[j]).astype(BF16)


def _dsa(qT, iqT, smT, ik, ckv, ckvT, wuk_bd, wuv_bd, tq=128):
    b, _, t = qT.shape
    assert t % DSA_KC == 0 and DSA_KC % tq == 0
    topk = min(TOPK_MAX, t // 4)
    nq = t // tq
    colq = lambda r: pl.BlockSpec((None, r, tq), lambda bi, i: (bi, 0, i))
    seq = lambda w: pl.BlockSpec((None, t, w), lambda bi, i: (bi, 0, 0))
    return pl.pallas_call(
        functools.partial(_dsa_body, topk=topk, tq=tq, idx_bits=int(t).bit_length()),
        grid=(b, nq),
        in_specs=[colq(512), colq(512), colq(16), seq(LANE), seq(KV_RANK),
                  pl.BlockSpec((None, t // DSA_KC, KV_EXT, DSA_KC), lambda bi, i: (bi, 0, 0, 0)),
                  _const_spec(wuk_bd.shape), _const_spec(wuv_bd.shape)],
        out_specs=pl.BlockSpec((None, tq, A_WIDTH), lambda bi, i: (bi, i, 0)),
        out_shape=jax.ShapeDtypeStruct((b, t, A_WIDTH), BF16),
        scratch_shapes=[pltpu.VMEM((t, tq), I32),
                        pltpu.VMEM((t, tq), I16),
                        pltpu.VMEM((t, tq), I16),
                        pltpu.VMEM((KV_RANK, A_HEADS * tq), BF16),
                        pltpu.VMEM((t, A_HEADS * tq), F32),
                        pltpu.VMEM((KV_EXT, A_HEADS * tq), F32),
                        pltpu.VMEM((SUBLANE, tq), I32)],
        compiler_params=_params("parallel", "arbitrary"), name="dsa",
    )(qT, iqT, smT, ik, ckv, ckvT, wuk_bd, wuv_bd)


def _s5_disc_body(are_ref, aim_ref, ldt_ref, bre_ref, bim_ref, abre_ref, abim_ref, bbre_ref, bbim_ref):
    a_re, a_im = are_ref[...], aim_ref[...]
    dt = jnp.exp(ldt_ref[...])
    mag = jnp.exp(dt * a_re)
    ang = dt * a_im
    ab_re = mag * jnp.cos(ang)
    ab_im = mag * jnp.sin(ang)
    den = a_re * a_re + a_im * a_im
    n_re = ab_re - 1.0
    f_re = (n_re * a_re + ab_im * a_im) / den
    f_im = (ab_im * a_re - n_re * a_im) / den
    abre_ref[...] = ab_re
    abim_ref[...] = ab_im
    b_re, b_im = bre_ref[...], bim_ref[...]
    bbre_ref[...] = f_re[:, None, :] * b_re - f_im[:, None, :] * b_im
    bbim_ref[...] = f_re[:, None, :] * b_im + f_im[:, None, :] * b_re


def _s5_discretise(a_re, a_im, log_dt, bT_re, bT_im):
    g, p = a_re.shape
    m = bT_re.shape[1]
    return pl.pallas_call(
        _s5_disc_body,
        out_shape=(jax.ShapeDtypeStruct((g, p), F32), jax.ShapeDtypeStruct((g, p), F32),
                   jax.ShapeDtypeStruct((g, m, p), F32), jax.ShapeDtypeStruct((g, m, p), F32)),
        name="s5_disc",
    )(a_re, a_im, log_dt.reshape(g, 1), bT_re, bT_im)


S5_BLK = 8
S5_NBLK = S5_GROUPS // S5_BLK
S5_HALF = S5_BLK * S5_STATE


def _s5_body(u_ref, bblk_ref, are_ref, aim_ref, cblk_ref, d_ref, wglu_ref, o_ref,
             hst_ref, xbuf_ref, y_ref, *, tt, nb):
    @pl.when(pl.program_id(0) == 0)
    def _():
        hst_ref[...] = jnp.zeros_like(hst_ref)

    for j in range(S5_NBLK):
        xbuf_ref[...] = _dot(u_ref[:, j * LANE:(j + 1) * LANE], bblk_ref[j])
        a_r = jnp.broadcast_to(are_ref[j], (nb, S5_HALF))
        a_i = jnp.broadcast_to(aim_ref[j], (nb, S5_HALF))

        def step(ti, carry):
            h_r, h_i = carry
            r0 = pl.multiple_of(ti * nb, nb)
            n_r = a_r * h_r - a_i * h_i + xbuf_ref[pl.ds(r0, nb), 0:S5_HALF]
            n_i = a_r * h_i + a_i * h_r + xbuf_ref[pl.ds(r0, nb), S5_HALF:2 * S5_HALF]
            xbuf_ref[pl.ds(r0, nb), 0:S5_HALF] = n_r
            xbuf_ref[pl.ds(r0, nb), S5_HALF:2 * S5_HALF] = n_i
            return n_r, n_i

        h_r, h_i = lax.fori_loop(0, tt, step, (hst_ref[j, 0], hst_ref[j, 1]), unroll=8)
        hst_ref[j, 0] = h_r
        hst_ref[j, 1] = h_i
        y_ref[:, j * LANE:(j + 1) * LANE] = _dot(xbuf_ref[...].astype(BF16), cblk_ref[j])

    y = y_ref[...] + d_ref[...] * u_ref[...].astype(F32)
    y = jax.nn.gelu(y)
    o_ref[...] = (y * jax.nn.sigmoid(_dot(y.astype(BF16), wglu_ref[...]))).astype(BF16)


def _s5(u_tm, bblk, a_re, a_im, cblk, d, wglu, nb, tt=128):
    rows, w = u_tm.shape
    t = rows // nb
    blk = pl.BlockSpec((tt * nb, w), lambda i: (i, 0))
    return pl.pallas_call(
        functools.partial(_s5_body, tt=tt, nb=nb),
        grid=(t // tt,),
        in_specs=[blk, _const_spec(bblk.shape), _const_spec(a_re.shape), _const_spec(a_im.shape),
                  _const_spec(cblk.shape), _const_spec((1, w)), _const_spec(wglu.shape)],
        out_specs=blk, out_shape=jax.ShapeDtypeStruct((rows, w), BF16),
        scratch_shapes=[pltpu.VMEM((S5_NBLK, 2, nb, S5_HALF), F32),
                        pltpu.VMEM((tt * nb, 2 * S5_HALF), F32),
                        pltpu.VMEM((tt * nb, w), F32)],
        compiler_params=_params("arbitrary"), name="s5",
    )(u_tm, bblk, a_re, a_im, cblk, d.reshape(1, w), wglu)


GDN_ROWS = 128
GDN_HALO = SUBLANE


def _gdn_body(qkv_ref, z_ref, gate_ref, cw_ref, alog_ref, dtb_ref, ng_ref, o_ref,
              tail_ref, xpad_ref, s_ref, *, nbatch):
    n = GDN_ROWS

    @pl.when(pl.program_id(1) == 0)
    def _():
        tail_ref[...] = jnp.zeros_like(tail_ref)
        s_ref[...] = jnp.zeros_like(s_ref)

    def softplus(v):
        return jnp.maximum(v, 0.0) + jnp.log1p(jnp.exp(-jnp.abs(v)))

    ri = lax.broadcasted_iota(I32, (n, n), 0)
    ci = lax.broadcasted_iota(I32, (n, n), 1)
    lower = ri >= ci
    strict = ri > ci
    chunk_shift = CHUNK.bit_length() - 1
    same = lax.shift_right_logical(ri, chunk_shift) == lax.shift_right_logical(ci, chunk_shift)
    triu = jnp.where(ri <= ci, 1.0, 0.0).astype(F32)
    eye = jnp.where(ri == ci, 1.0, 0.0).astype(F32)
    eye_b = eye.astype(BF16)
    ones = jnp.ones((n, n), BF16)

    def row_sum(a):
        return _dot(a.astype(BF16), ones)

    def split3(a):
        hi = a.astype(BF16).astype(F32)
        mid = (a - hi).astype(BF16).astype(F32)
        return hi, mid, (a - hi - mid).astype(BF16).astype(F32)

    chains = [(bb, h) for bb in range(nbatch) for h in range(C_HEADS)]
    hs = lambda h: slice(h * C_HEAD_DIM, (h + 1) * C_HEAD_DIM)
    q, k, v, gc, gr, beta = {}, {}, {}, {}, {}, {}
    for bb in range(nbatch):
        xpad_ref[bb, 0:GDN_HALO, :] = tail_ref[bb]
        xpad_ref[bb, GDN_HALO:GDN_HALO + n, :] = qkv_ref[bb].astype(F32)
        tail_ref[bb] = xpad_ref[bb, n:n + GDN_HALO, :]
        xfull_rows = xpad_ref[bb]
        x = cw_ref[C_CONV - 1:C_CONV, :] * xfull_rows[GDN_HALO:GDN_HALO + n]
        for j in range(C_CONV - 1):
            shifted = pltpu.roll(xfull_rows, C_CONV - 1 - j, axis=0)
            x = x + cw_ref[j:j + 1, :] * shifted[GDN_HALO:GDN_HALO + n]
        x = _silu(x)
        gates = gate_ref[bb]
        g_rows = -jnp.exp(alog_ref[...]) * softplus(gates + dtb_ref[...])
        gam_rows = _dot_f32(g_rows, triu)
        row_terms = split3(jnp.where(lax.broadcasted_iota(I32, (16, n), 0) < SM_GB,
                                     gam_rows, jax.nn.sigmoid(gates)))
        for h in range(C_HEADS):
            c = (bb, h)
            spread = None
            for term in row_terms:
                rows = jnp.concatenate([jnp.broadcast_to(term[SM_GA + h:SM_GA + h + 1, :], (n, n)),
                                        jnp.broadcast_to(term[SM_GB + h:SM_GB + h + 1, :], (n, n))], axis=0)
                part = _dot_nt(eye_b, rows.astype(BF16))
                spread = part if spread is None else spread + part
            gc[c] = spread[:, :n]
            beta[c] = spread[:, n:]
            gr[c] = gam_rows[SM_GA + h:SM_GA + h + 1, :]
            qh = x[:, hs(h)]
            kh = x[:, C_WIDTH + h * C_HEAD_DIM:C_WIDTH + (h + 1) * C_HEAD_DIM]
            q[c] = qh * lax.rsqrt(row_sum(qh * qh) + EPS) * (C_HEAD_DIM ** -0.5)
            k[c] = kh * lax.rsqrt(row_sum(kh * kh) + EPS)
            v[c] = x[:, 2 * C_WIDTH + h * C_HEAD_DIM:2 * C_WIDTH + (h + 1) * C_HEAD_DIM]

    decay = {c: jnp.exp(jnp.where(lower, gc[c] - gr[c], MASKED)) for c in chains}
    kb = {c: k[c] * beta[c] for c in chains}
    kbf = {c: k[c].astype(BF16) for c in chains}
    m_full = {c: jnp.where(strict, _dot_nt(kb[c].astype(BF16), kbf[c]) * decay[c], 0.0) for c in chains}
    attn = {c: (_dot_nt(q[c].astype(BF16), kbf[c]) * decay[c]).astype(BF16) for c in chains}
    m_bd = {c: jnp.where(same, m_full[c], 0.0) for c in chains}
    m_off = {c: (m_full[c] - m_bd[c]).astype(BF16) for c in chains}
    xinv = {c: eye - m_bd[c] for c in chains}
    pw = {}
    for c in chains:
        mb = m_bd[c].astype(BF16)
        pw[c] = _dot(mb, mb)
    for lvl in range(5):
        pwb = {c: pw[c].astype(BF16) for c in chains}
        xinv = {c: xinv[c] + _dot(xinv[c].astype(BF16), pwb[c]) for c in chains}
        if lvl < 4:
            pw = {c: _dot(pwb[c], pwb[c]) for c in chains}
    xb = {c: xinv[c].astype(BF16) for c in chains}
    xm = {c: _dot(xb[c], m_off[c]).astype(BF16) for c in chains}
    xfull = {c: (xinv[c] - _dot(xm[c], xb[c])).astype(BF16) for c in chains}
    eg = {c: jnp.exp(gc[c]) for c in chains}
    sol = {c: _dot(xfull[c], jnp.concatenate([v[c] * beta[c], kb[c] * eg[c]], axis=1).astype(BF16))
           for c in chains}
    g_last = {c: gc[c][n - 1:n, :] for c in chains}
    q_dec = {c: (q[c] * eg[c]).astype(BF16) for c in chains}
    k_decT = {c: (k[c] * jnp.exp(g_last[c] - gc[c])).T.astype(BF16) for c in chains}
    s_old = {c: s_ref[c[0], c[1]] for c in chains}
    sb = {c: s_old[c].astype(BF16) for c in chains}
    vb = {c: (sol[c][:, :C_HEAD_DIM] - _dot(sol[c][:, C_HEAD_DIM:].astype(BF16), sb[c])).astype(BF16)
          for c in chains}
    o = {c: _dot(q_dec[c], sb[c]) + _dot(attn[c], vb[c]) for c in chains}
    for c in chains:
        bb, h = c
        s_ref[bb, h] = s_old[c] * jnp.exp(g_last[c]) + _dot(k_decT[c], vb[c])
        norm = lax.rsqrt(row_sum(o[c] * o[c]) * (1.0 / C_HEAD_DIM) + EPS)
        out = o[c] * norm * ng_ref[...] * _silu(z_ref[bb, :, hs(h)].astype(F32))
        o_ref[bb, :, hs(h)] = out.astype(BF16)


def _gdn(qkv, z, smT, conv_w, a_log, dt_bias, norm_g, nbatch=2):
    b, t, _ = qkv.shape
    n = GDN_ROWS
    assert b % nbatch == 0
    row = lambda w: pl.BlockSpec((nbatch, n, w), lambda bi, i: (bi, i, 0))
    pad_col = lambda v: jnp.zeros((16, 1), F32).at[SM_GA:SM_GA + C_HEADS, 0].set(v)
    return pl.pallas_call(
        functools.partial(_gdn_body, nbatch=nbatch),
        grid=(b // nbatch, t // n),
        in_specs=[row(3 * C_WIDTH), row(C_WIDTH),
                  pl.BlockSpec((nbatch, 16, n), lambda bi, i: (bi, 0, i)),
                  _const_spec((C_CONV, 3 * C_WIDTH)),
                  _const_spec((16, 1)), _const_spec((16, 1)), _const_spec((1, C_HEAD_DIM))],
        out_specs=row(C_WIDTH), out_shape=jax.ShapeDtypeStruct((b, t, C_WIDTH), BF16),
        scratch_shapes=[pltpu.VMEM((nbatch, GDN_HALO, 3 * C_WIDTH), F32),
                        pltpu.VMEM((nbatch, GDN_HALO + n, 3 * C_WIDTH), F32),
                        pltpu.VMEM((nbatch, C_HEADS, C_HEAD_DIM, C_HEAD_DIM), F32)],
        compiler_params=_params("parallel", "arbitrary"), name="gdn",
    )(qkv, z, smT, conv_w, pad_col(a_log), pad_col(dt_bias), norm_g.reshape(1, C_HEAD_DIM))


def _merge_body(x_ref, oa_ref, yb_ref, oc_ref, gate_ref, wpa_ref, wpb_ref, wpc_ref, wo_ref, o_ref):
    d = D_MODEL
    mix = jax.nn.sigmoid(gate_ref[:, 0:d].astype(F32)) * _dot(oa_ref[...], wpa_ref[...])
    mix = mix + jax.nn.sigmoid(gate_ref[:, d:2 * d].astype(F32)) * _dot(yb_ref[...], wpb_ref[...])
    mix = mix + jax.nn.sigmoid(gate_ref[:, 2 * d:3 * d].astype(F32)) * _dot(oc_ref[...], wpc_ref[...])
    o_ref[...] = x_ref[...] + _dot(mix.astype(BF16), wo_ref[...])


def _merge(x, oa, yb, oc, gates, wpa, wpb, wpc, wo, tm=512):
    b, t, d = x.shape
    row = lambda w: pl.BlockSpec((None, tm, w), lambda bi, i: (bi, i, 0))
    time_major = pl.BlockSpec((tm, S5_WIDTH), lambda bi, i: (i, bi))
    return pl.pallas_call(
        _merge_body,
        grid=(b, t // tm),
        in_specs=[row(d), row(A_WIDTH), time_major, row(C_WIDTH), row(N_BRANCH * d),
                  _const_spec(wpa.shape), _const_spec(wpb.shape), _const_spec(wpc.shape),
                  _const_spec(wo.shape)],
        out_specs=row(d), out_shape=jax.ShapeDtypeStruct((b, t, d), F32),
        compiler_params=_params("parallel", "parallel"), name="merge",
    )(x, oa, yb, oc, gates, wpa, wpb, wpc, wo)


FFN_TILE = 256
FFN_HALO = 2 * SUBLANE


def _ffn_body(x_ref, xprev_ref, g_ref, wup_ref, cw_ref, wdn_ref, fg_ref, o_ref, hcat_ref, hid_ref, acc_ref,
              *, tm, final_norm):
    first = pl.program_id(1) == 0
    hcat_ref[0:FFN_HALO, :] = _rms(xprev_ref[...], g_ref[...]).astype(BF16)
    hcat_ref[FFN_HALO:FFN_HALO + tm, :] = _rms(x_ref[...], g_ref[...]).astype(BF16)
    hb = hcat_ref[...]
    keep = jnp.where(jnp.logical_and(
        first, lax.broadcasted_iota(I32, (FFN_HALO + tm, 1), 0) < FFN_HALO), 0.0, 1.0)

    def conv(cols):
        hid_ref[...] = _dot(hb, wup_ref[:, cols]) * keep
        base = FFN_HALO - FFN_CONV + 1
        out = cw_ref[0:1, cols] * hid_ref[pl.ds(base, tm), :]
        for j in range(1, FFN_CONV):
            out = out + cw_ref[j:j + 1, cols] * hid_ref[pl.ds(base + j, tm), :]
        return out

    acc_ref[...] = jnp.zeros_like(acc_ref)
    for j in range(D_FF // FFN_TILE):
        gate = conv(slice(j * FFN_TILE, (j + 1) * FFN_TILE))
        val = conv(slice(D_FF + j * FFN_TILE, D_FF + (j + 1) * FFN_TILE))
        act = (_silu(gate) * val).astype(BF16)
        acc_ref[...] += _dot(act, wdn_ref[j * FFN_TILE:(j + 1) * FFN_TILE, :])
    y = x_ref[...] + acc_ref[...]
    if final_norm:
        y = _rms(y, fg_ref[...])
    o_ref[...] = y


def _ffn(x, g, wup, cw, wdn, final_g, final_norm, tm=512):
    b, t, d = x.shape
    row = pl.BlockSpec((None, tm, d), lambda bi, i: (bi, i, 0))
    halo_blocks = tm // FFN_HALO
    prev = pl.BlockSpec((None, FFN_HALO, d), lambda bi, i: (bi, jnp.maximum(i * halo_blocks - 1, 0), 0))
    return pl.pallas_call(
        functools.partial(_ffn_body, tm=tm, final_norm=final_norm),
        grid=(b, t // tm),
        in_specs=[row, prev, _const_spec((1, d)), _const_spec(wup.shape), _const_spec(cw.shape),
                  _const_spec(wdn.shape), _const_spec((1, d))],
        out_specs=row, out_shape=jax.ShapeDtypeStruct((b, t, d), F32),
        scratch_shapes=[pltpu.VMEM((FFN_HALO + tm, d), BF16),
                        pltpu.VMEM((FFN_HALO + tm, FFN_TILE), F32),
                        pltpu.VMEM((tm, d), F32)],
        compiler_params=_params("parallel", "parallel"), name="ffn",
    )(x, x, g.reshape(1, d), wup, cw, wdn, final_g.reshape(1, d))


def _pair_block_diag(w):
    h, r, c = w.shape
    w = w.reshape(h // 2, 2, r, c)
    z = jnp.zeros((h // 2, r, c), w.dtype)
    top = jnp.concatenate([w[:, 0], z], axis=2)
    bot = jnp.concatenate([z, w[:, 1]], axis=2)
    return jnp.concatenate([top, bot], axis=1)


def _pack_in_proj(w_in):
    o = np.cumsum((0,) + IN_SPLITS)
    part = lambda i: w_in[:, o[i]:o[i + 1]]
    w_q, w_ckv, w_iq, w_ik, w_iw, w_s5, w_qkv, w_z, w_ca, w_cb, w_gate = (part(i) for i in range(11))
    d = w_in.shape[0]
    iq_t = w_iq.T.reshape(IDX_HEADS, IDX_DIM, d)
    iq_t = jnp.pad(iq_t, ((0, 0), (0, LANE - IDX_DIM), (0, 0))).reshape(IDX_HEADS * LANE, d)
    small = jnp.zeros((16, d), w_in.dtype)
    small = small.at[SM_IW:SM_IW + IDX_HEADS].set(w_iw.T).at[SM_GA:SM_GA + C_HEADS].set(w_ca.T)
    small = small.at[SM_GB:SM_GB + C_HEADS].set(w_cb.T)
    wt = jnp.concatenate([w_q.T, iq_t, w_ckv.T, small], axis=0).astype(BF16)
    wn = jnp.concatenate([w_ckv, jnp.pad(w_ik, ((0, 0), (0, LANE - IDX_DIM))), w_s5, w_qkv, w_z,
                          w_gate], axis=1).astype(BF16)
    assert wt.shape[0] == WT_ROWS and wn.shape[1] == WN_COLS
    return wt, wn


def _pack_s5(bb_re, bb_im, ab_re, ab_im, c_re, c_im):
    eye = jnp.eye(S5_BLK, dtype=F32)
    def in_map(bbT):
        w = bbT.reshape(S5_NBLK, S5_BLK, S5_GROUP, S5_STATE)
        return jnp.einsum('jgmp,gh->jgmhp', w, eye).reshape(S5_NBLK, S5_BLK * S5_GROUP, S5_HALF)
    def out_map(c):
        w = c.reshape(S5_NBLK, S5_BLK, S5_GROUP, S5_STATE)
        return jnp.einsum('jgmp,gh->jgphm', w, eye).reshape(S5_NBLK, S5_HALF, S5_BLK * S5_GROUP)
    bblk = jnp.concatenate([in_map(bb_re), in_map(bb_im)], axis=2).astype(BF16)
    cblk = jnp.concatenate([out_map(c_re), out_map(-c_im)], axis=1).astype(BF16)
    return bblk, ab_re.reshape(S5_NBLK, 1, S5_HALF), ab_im.reshape(S5_NBLK, 1, S5_HALF), cblk


def kernel(x, attn_norm_g, w_in, kv_norm_g, w_uk, w_uv, w_proj_a, s5_a_re, s5_a_im, s5_log_dt, s5_b_re,
           s5_b_im, s5_c_re, s5_c_im, s5_d, w_glu, w_proj_b, gdn_conv_w, gdn_a_log, gdn_dt_bias,
           gdn_norm_g, w_proj_c, w_out, ffn_norm_g, w_up, ffn_conv_w, w_down, final_norm_g):
    b, t, d = x.shape
    depth = w_in.shape[0]
    for l in range(depth):
        wt, wn = _pack_in_proj(w_in[l])
        (qT, iqT, ckvT, smT, ckv, ik, u, qkv, z, gates) = _in_proj(x, attn_norm_g[l], wt, wn, kv_norm_g[l])

        wuk_bd = _pair_block_diag(
            jnp.transpose(w_uk[l], (1, 0, 2)) * (A_HEAD_DIM ** -0.5 * math.log2(math.e))).astype(BF16)
        wuv_bd = _pair_block_diag(jnp.transpose(w_uv[l], (1, 0, 2))).astype(BF16)
        oa = _dsa(qT, iqT, smT, ik, ckv, ckvT, wuk_bd, wuv_bd)

        ab_re, ab_im, bb_re, bb_im = _s5_discretise(
            s5_a_re[l], s5_a_im[l], s5_log_dt[l],
            jnp.transpose(s5_b_re[l], (0, 2, 1)), jnp.transpose(s5_b_im[l], (0, 2, 1)))
        bblk, a_r, a_i, cblk = _pack_s5(bb_re, bb_im, ab_re, ab_im, s5_c_re[l], s5_c_im[l])
        yb = _s5(u.reshape(t * b, S5_WIDTH), bblk, a_r, a_i, cblk, s5_d[l], w_glu[l].astype(BF16), nb=b)
        yb = yb.reshape(t, b * S5_WIDTH)

        oc = _gdn(qkv, z, smT, gdn_conv_w[l], gdn_a_log[l], gdn_dt_bias[l], gdn_norm_g[l])

        x = _merge(x, oa, yb, oc, gates, w_proj_a[l].astype(BF16), w_proj_b[l].astype(BF16),
                   w_proj_c[l].astype(BF16), w_out[l].astype(BF16))
        x = _ffn(x, ffn_norm_g[l], w_up[l].astype(BF16), ffn_conv_w[l], w_down[l].astype(BF16),
                 final_norm_g, final_norm=(l == depth - 1))
    return x
```

```python
import functools
import math

import numpy as np
import jax
import jax.numpy as jnp
from jax import lax
from jax.experimental import pallas as pl
from jax.experimental.pallas import tpu as pltpu

F32 = jnp.float32
BF16 = jnp.bfloat16
I32 = jnp.int32
I16 = jnp.int16

D_MODEL = 1024
EPS = 1e-6
CHUNK = 64
A_HEADS, A_HEAD_DIM, KV_RANK = 8, 64, 128
A_WIDTH = A_HEADS * A_HEAD_DIM
IDX_HEADS, IDX_DIM = 4, 64
TOPK_MAX = 256
S5_WIDTH, S5_GROUP, S5_STATE = 512, 16, 64
S5_GROUPS = S5_WIDTH // S5_GROUP
C_HEADS, C_HEAD_DIM, C_CONV = 4, 128, 4
C_WIDTH = C_HEADS * C_HEAD_DIM
D_FF, FFN_CONV = 2816, 3
N_BRANCH = 3
IN_SPLITS = (A_WIDTH, KV_RANK, IDX_HEADS * IDX_DIM, IDX_DIM, IDX_HEADS, S5_WIDTH,
             3 * C_WIDTH, C_WIDTH, C_HEADS, C_HEADS, N_BRANCH * D_MODEL)

LANE = 128
SUBLANE = 8
VMEM_LIMIT = 56 * 1024 * 1024
MASKED = -1e30
INT_MIN = -2 ** 31
DSA_KC = 512
NO_CUT = 2 ** 30
KV_EXT = KV_RANK + 16
HALF16 = 2 ** 15

WT_Q, WT_IQ, WT_CKV, WT_SMALL = 0, 512, 1024, 1152
WT_ROWS = 1168
SM_IW, SM_GA, SM_GB = 0, 8, 12
WN_CKV, WN_IK, WN_S5, WN_QKV, WN_Z, WN_GATE, WN_COLS = 0, 128, 256, 768, 2304, 2816, 5888


def _dot(a, b):
    return jnp.dot(a, b, preferred_element_type=F32)


def _dot_nt(a, b):
    return lax.dot_general(a, b, (((1,), (1,)), ((), ())), preferred_element_type=F32)


def _dot_f32(a, b):
    return jnp.dot(a, b, preferred_element_type=F32, precision=lax.Precision.HIGHEST)


def _rms(x, g):
    return x * lax.rsqrt(jnp.mean(x * x, axis=-1, keepdims=True) + EPS) * g


def _silu(x):
    return x * jax.nn.sigmoid(x)


def _const_spec(shape):
    nd = len(shape)
    return pl.BlockSpec(shape, lambda *_: (0,) * nd, pipeline_mode=pl.Buffered(1))


def _params(*sem):
    return pltpu.CompilerParams(dimension_semantics=sem, vmem_limit_bytes=VMEM_LIMIT)


def _in_proj_body(x_ref, g_ref, wt_ref, wn_ref, kvg_col_ref, kvg_row_ref,
                  qT_ref, iqT_ref, ckvT_ref, smT_ref, ckv_ref, ik_ref, u_ref, qkv_ref, z_ref,
                  gate_ref, *, tm):
    hb = _rms(x_ref[...], g_ref[...]).astype(BF16)
    qT_ref[...] = _dot_nt(wt_ref[WT_Q:WT_IQ, :], hb).astype(BF16)
    iqT_ref[...] = _dot_nt(wt_ref[WT_IQ:WT_CKV, :], hb).astype(BF16)
    cT = _dot_nt(wt_ref[WT_CKV:WT_SMALL, :], hb)
    cT = cT * lax.rsqrt(jnp.mean(cT * cT, axis=0, keepdims=True) + EPS) * kvg_col_ref[...]
    for j in range(tm // DSA_KC):
        ckvT_ref[j, 0:KV_RANK, :] = cT[:, j * DSA_KC:(j + 1) * DSA_KC].astype(BF16)
        ckvT_ref[j, KV_RANK:KV_EXT, :] = jnp.ones((KV_EXT - KV_RANK, DSA_KC), BF16)
    smT_ref[...] = _dot_nt(wt_ref[WT_SMALL:WT_ROWS, :], hb)
    c = _dot(hb, wn_ref[:, WN_CKV:WN_IK])
    ckv_ref[...] = _rms(c, kvg_row_ref[...]).astype(BF16)
    ik_ref[...] = _dot(hb, wn_ref[:, WN_IK:WN_S5]).astype(BF16)
    u_ref[...] = _dot(hb, wn_ref[:, WN_S5:WN_QKV]).astype(BF16)
    for j in range(3):
        lo = WN_QKV + j * C_WIDTH
        qkv_ref[:, j * C_WIDTH:(j + 1) * C_WIDTH] = _dot(hb, wn_ref[:, lo:lo + C_WIDTH]).astype(BF16)
    z_ref[...] = _dot(hb, wn_ref[:, WN_Z:WN_GATE]).astype(BF16)
    for j in range(2 * N_BRANCH):
        lo = WN_GATE + j * 512
        gate_ref[:, j * 512:(j + 1) * 512] = _dot(hb, wn_ref[:, lo:lo + 512]).astype(BF16)


def _in_proj(x, g, wt, wn, kvg, tm=DSA_KC):
    b, t, d = x.shape
    nt = t // tm
    row = lambda w: pl.BlockSpec((None, tm, w), lambda bi, i: (bi, i, 0))
    col = lambda r: pl.BlockSpec((None, r, tm), lambda bi, i: (bi, 0, i))
    out_shape = (
        jax.ShapeDtypeStruct((b, 512, t), BF16),
        jax.ShapeDtypeStruct((b, 512, t), BF16),
        jax.ShapeDtypeStruct((b, t // DSA_KC, KV_EXT, DSA_KC), BF16),
        jax.ShapeDtypeStruct((b, 16, t), F32),
        jax.ShapeDtypeStruct((b, t, KV_RANK), BF16),
        jax.ShapeDtypeStruct((b, t, LANE), BF16),
        jax.ShapeDtypeStruct((t, b * S5_WIDTH), BF16),
        jax.ShapeDtypeStruct((b, t, 3 * C_WIDTH), BF16),
        jax.ShapeDtypeStruct((b, t, C_WIDTH), BF16),
        jax.ShapeDtypeStruct((b, t, N_BRANCH * D_MODEL), BF16),
    )
    out_specs = (
        col(512), col(512),
        pl.BlockSpec((None, tm // DSA_KC, KV_EXT, DSA_KC), lambda bi, i: (bi, i, 0, 0)),
        col(16), row(KV_RANK), row(LANE),
        pl.BlockSpec((tm, S5_WIDTH), lambda bi, i: (i, bi)),
        row(3 * C_WIDTH), row(C_WIDTH), row(N_BRANCH * D_MODEL),
    )
    assert tm % DSA_KC == 0
    return pl.pallas_call(
        functools.partial(_in_proj_body, tm=tm),
        grid=(b, nt),
        in_specs=[row(d), _const_spec((1, d)), _const_spec(wt.shape), _const_spec(wn.shape),
                  _const_spec((KV_RANK, 1)), _const_spec((1, KV_RANK))],
        out_specs=out_specs, out_shape=out_shape,
        compiler_params=_params("parallel", "parallel"), name="in_proj",
    )(x, g.reshape(1, d), wt, wn, kvg.reshape(KV_RANK, 1), kvg.reshape(1, KV_RANK))


def _dsa_body(qT_ref, iqT_ref, smT_ref, ik_ref, ckv_ref, ckvT_ref, wuk_ref, wuv_ref, o_ref,
              key_ref, hi_ref, lo_ref, qlT_ref, lg_ref, acc_ref, sel_ref, *, topk, tq, idx_bits):
    kc_rows = DSA_KC
    hq = A_HEADS * tq
    qi = pl.program_id(1)
    n_kc = (qi * tq + tq + kc_rows - 1) // kc_rows

    def rows_of(kc):
        return pl.ds(pl.multiple_of(kc * kc_rows, kc_rows), kc_rows)

    for j in range(A_HEADS // 2):
        ql = _dot(wuk_ref[j], qT_ref[j * LANE:(j + 1) * LANE, :]).astype(BF16)
        qlT_ref[:, (2 * j) * tq:(2 * j + 1) * tq] = ql[0:KV_RANK]
        qlT_ref[:, (2 * j + 1) * tq:(2 * j + 2) * tq] = ql[KV_RANK:2 * KV_RANK]

    iw = smT_ref[SM_IW:SM_IW + IDX_HEADS, :] * (IDX_HEADS ** -0.5 * IDX_DIM ** -0.5)
    chunk_shift = CHUNK.bit_length() - 1
    q_chunk = lax.shift_right_logical(qi * tq + lax.broadcasted_iota(I32, (kc_rows, tq), 1), chunk_shift)
    row_iota = lax.broadcasted_iota(I32, (kc_rows, tq), 0)

    def score_body(kc, carry):
        ikc = ik_ref[rows_of(kc), :]
        sc = jnp.zeros((kc_rows, tq), F32)
        for h in range(IDX_HEADS):
            rel = jnp.maximum(_dot(ikc, iqT_ref[h * LANE:(h + 1) * LANE, :]), 0.0)
            sc = sc + rel * iw[h:h + 1, :]
        bits = lax.bitcast_convert_type(sc, I32)
        mag = bits & jnp.int32(0x7FFFFFFF)
        key = jnp.where(bits < 0, -mag, mag)
        adm = lax.shift_right_logical(row_iota + kc * kc_rows, chunk_shift) <= q_chunk
        key = jnp.where(adm, key, jnp.int32(INT_MIN))
        key_ref[rows_of(kc), :] = key
        hi_ref[rows_of(kc), :] = lax.shift_right_arithmetic(key, 16).astype(I16)
        lo_ref[rows_of(kc), :] = ((key & jnp.int32(0xFFFF)) - jnp.int32(HALF16)).astype(I16)
        return carry

    lax.fori_loop(0, n_kc, score_body, 0)

    def col_count(pred):
        lanes = 4 * SUBLANE
        def body(kc, acc):
            m = pred(key_ref[rows_of(kc), :], kc * kc_rows)
            mi = jnp.where(m, jnp.int32(1), jnp.int32(0))
            return acc + jnp.sum(mi.reshape(kc_rows // lanes, lanes, tq), axis=0)
        acc = lax.fori_loop(0, n_kc, body, jnp.zeros((lanes, tq), I32))
        return jnp.sum(acc, axis=0, keepdims=True)

    def col_count16(src_ref, pred):
        rows = 4 * SUBLANE
        def body(kc, acc):
            ones = jnp.where(pred(src_ref[rows_of(kc), :]), jnp.ones((), BF16), jnp.zeros((), BF16))
            parts = [ones[r * rows:(r + 1) * rows] for r in range(kc_rows // rows)]
            while len(parts) > 1:
                parts = [parts[i] + parts[i + 1] for i in range(0, len(parts), 2)]
            return acc + parts[0].astype(F32)
        acc = lax.fori_loop(0, n_kc, body, jnp.zeros((rows, tq), F32))
        return jnp.sum(acc, axis=0, keepdims=True).astype(I32)

    def kth_largest16(src_ref, rank):
        def bit_body(i, thr):
            cand = thr + lax.shift_left(jnp.int32(1), jnp.int32(15) - i)
            cand16 = cand.astype(I16)
            cnt = col_count16(src_ref, lambda v: v >= cand16)
            return jnp.where(cnt >= rank, cand, thr)
        return lax.fori_loop(0, 16, bit_body, jnp.full((1, tq), -HALF16, I32))

    sel_ref[0:1, :] = jnp.full((1, tq), INT_MIN, I32)
    sel_ref[1:2, :] = jnp.full((1, tq), -1, I32)

    @pl.when((qi + 1) * tq > topk)
    def _():
        thr_hi = kth_largest16(hi_ref, topk)
        thr_hi16 = thr_hi.astype(I16)
        rank_lo = topk - col_count16(hi_ref, lambda v: v > thr_hi16)

        def bucket_body(kc, carry):
            in_bucket = hi_ref[rows_of(kc), :] == thr_hi16
            lo_ref[rows_of(kc), :] = jnp.where(in_bucket, lo_ref[rows_of(kc), :], jnp.int16(-HALF16))
            return carry
        lax.fori_loop(0, n_kc, bucket_body, 0)
        thr_lo = kth_largest16(lo_ref, rank_lo)
        thr = lax.shift_left(thr_hi, 16) + (thr_lo + jnp.int32(HALF16))
        open_thr = thr != INT_MIN
        need = topk - col_count(lambda k, r0: k > thr)
        surplus = jnp.logical_and(col_count(lambda k, r0: k == thr) > need, open_thr)
        sel_ref[0:1, :] = thr
        sel_ref[1:2, :] = jnp.where(open_thr, jnp.int32(NO_CUT), jnp.int32(-1))

        @pl.when(jnp.max(jnp.where(surplus, 1, 0)) > 0)
        def _():
            def idx_body(i, jcut):
                cand = jcut + lax.shift_left(jnp.int32(1), jnp.int32(idx_bits - 1) - i)
                f = col_count(lambda k, r0: jnp.logical_and(k == thr, row_iota + r0 < cand))
                return jnp.where(f < need, cand, jcut)
            jcut = lax.fori_loop(0, idx_bits, idx_body, jnp.zeros((1, tq), I32))
            sel_ref[1:2, :] = jnp.where(open_thr, jcut, jnp.int32(-1))

    thr = sel_ref[0:1, :]
    jcut = sel_ref[1:2, :]

    def logit_body(kc, m8):
        key = key_ref[rows_of(kc), :]
        sel = jnp.logical_or(key > thr, jnp.logical_and(key == thr, row_iota + kc * kc_rows <= jcut))
        bias = jnp.where(sel, 0.0, MASKED)
        lg = _dot(ckv_ref[rows_of(kc), :], qlT_ref[...])
        parts = []
        for h in range(A_HEADS):
            x = lg[:, h * tq:(h + 1) * tq] + bias
            lg_ref[rows_of(kc), h * tq:(h + 1) * tq] = x
            parts.append(jnp.max(x.reshape(kc_rows // SUBLANE, SUBLANE, tq), axis=0))
        return jnp.maximum(m8, jnp.concatenate(parts, axis=1))

    m8 = lax.fori_loop(0, n_kc, logit_body, jnp.full((SUBLANE, hq), MASKED, F32))
    m = jnp.max(m8, axis=0, keepdims=True)

    acc_ref[...] = jnp.zeros_like(acc_ref)

    def pv_body(kc, carry):
        p = jnp.exp2(lg_ref[rows_of(kc), :] - m)
        acc_ref[...] += _dot(ckvT_ref[kc], p.astype(BF16))
        return carry

    lax.fori_loop(0, n_kc, pv_body, 0)
    l = acc_ref[KV_RANK:KV_RANK + 1, :]

    for j in range(A_HEADS // 2):
        parts = []
        for h in (2 * j, 2 * j + 1):
            o_latT = acc_ref[0:KV_RANK, h * tq:(h + 1) * tq] / l[:, h * tq:(h + 1) * tq]
            parts.append(o_latT.T)
        o_lat = jnp.concatenate(parts, axis=1).astype(BF16)
        o_ref[:, j * LANE:(j + 1) * LANE] = _dot(o_lat, wuv_ref[j]).astype(BF16)


def _dsa(qT, iqT, smT, ik, ckv, ckvT, wuk_bd, wuv_bd, tq=128):
    b, _, t = qT.shape
    assert t % DSA_KC == 0 and DSA_KC % tq == 0
    topk = min(TOPK_MAX, t // 4)
    nq = t // tq
    colq = lambda r: pl.BlockSpec((None, r, tq), lambda bi, i: (bi, 0, i))
    seq = lambda w: pl.BlockSpec((None, t, w), lambda bi, i: (bi, 0, 0))
    return pl.pallas_call(
        functools.partial(_dsa_body, topk=topk, tq=tq, idx_bits=int(t).bit_length()),
        grid=(b, nq),
        in_specs=[colq(512), colq(512), colq(16), seq(LANE), seq(KV_RANK),
                  pl.BlockSpec((None, t // DSA_KC, KV_EXT, DSA_KC), lambda bi, i: (bi, 0, 0, 0)),
                  _const_spec(wuk_bd.shape), _const_spec(wuv_bd.shape)],
        out_specs=pl.BlockSpec((None, tq, A_WIDTH), lambda bi, i: (bi, i, 0)),
        out_shape=jax.ShapeDtypeStruct((b, t, A_WIDTH), BF16),
        scratch_shapes=[pltpu.VMEM((t, tq), I32),
                        pltpu.VMEM((t, tq), I16),
                        pltpu.VMEM((t, tq), I16),
                        pltpu.VMEM((KV_RANK, A_HEADS * tq), BF16),
                        pltpu.VMEM((t, A_HEADS * tq), F32),
                        pltpu.VMEM((KV_EXT, A_HEADS * tq), F32),
                        pltpu.VMEM((SUBLANE, tq), I32)],
        compiler_params=_params("parallel", "arbitrary"), name="dsa",
    )(qT, iqT, smT, ik, ckv, ckvT, wuk_bd, wuv_bd)


def _s5_disc_body(are_ref, aim_ref, ldt_ref, bre_ref, bim_ref, abre_ref, abim_ref, bbre_ref, bbim_ref):
    a_re, a_im = are_ref[...], aim_ref[...]
    dt = jnp.exp(ldt_ref[...])
    mag = jnp.exp(dt * a_re)
    ang = dt * a_im
    ab_re = mag * jnp.cos(ang)
    ab_im = mag * jnp.sin(ang)
    den = a_re * a_re + a_im * a_im
    n_re = ab_re - 1.0
    f_re = (n_re * a_re + ab_im * a_im) / den
    f_im = (ab_im * a_re - n_re * a_im) / den
    abre_ref[...] = ab_re
    abim_ref[...] = ab_im
    b_re, b_im = bre_ref[...], bim_ref[...]
    bbre_ref[...] = f_re[:, None, :] * b_re - f_im[:, None, :] * b_im
    bbim_ref[...] = f_re[:, None, :] * b_im + f_im[:, None, :] * b_re


def _s5_discretise(a_re, a_im, log_dt, bT_re, bT_im):
    g, p = a_re.shape
    m = bT_re.shape[1]
    return pl.pallas_call(
        _s5_disc_body,
        out_shape=(jax.ShapeDtypeStruct((g, p), F32), jax.ShapeDtypeStruct((g, p), F32),
                   jax.ShapeDtypeStruct((g, m, p), F32), jax.ShapeDtypeStruct((g, m, p), F32)),
        name="s5_disc",
    )(a_re, a_im, log_dt.reshape(g, 1), bT_re, bT_im)


S5_BLK = 8
S5_NBLK = S5_GROUPS // S5_BLK
S5_HALF = S5_BLK * S5_STATE


def _s5_body(u_ref, bblk_ref, are_ref, aim_ref, cblk_ref, d_ref, wglu_ref, o_ref,
             hst_ref, xbuf_ref, y_ref, *, tt, nb):
    @pl.when(pl.program_id(0) == 0)
    def _():
        hst_ref[...] = jnp.zeros_like(hst_ref)

    for j in range(S5_NBLK):
        xbuf_ref[...] = _dot(u_ref[:, j * LANE:(j + 1) * LANE], bblk_ref[j])
        a_r = jnp.broadcast_to(are_ref[j], (nb, S5_HALF))
        a_i = jnp.broadcast_to(aim_ref[j], (nb, S5_HALF))

        def step(ti, carry):
            h_r, h_i = carry
            r0 = pl.multiple_of(ti * nb, nb)
            n_r = a_r * h_r - a_i * h_i + xbuf_ref[pl.ds(r0, nb), 0:S5_HALF]
            n_i = a_r * h_i + a_i * h_r + xbuf_ref[pl.ds(r0, nb), S5_HALF:2 * S5_HALF]
            xbuf_ref[pl.ds(r0, nb), 0:S5_HALF] = n_r
            xbuf_ref[pl.ds(r0, nb), S5_HALF:2 * S5_HALF] = n_i
            return n_r, n_i

        h_r, h_i = lax.fori_loop(0, tt, step, (hst_ref[j, 0], hst_ref[j, 1]), unroll=8)
        hst_ref[j, 0] = h_r
        hst_ref[j, 1] = h_i
        y_ref[:, j * LANE:(j + 1) * LANE] = _dot(xbuf_ref[...].astype(BF16), cblk_ref[j])

    y = y_ref[...] + d_ref[...] * u_ref[...].astype(F32)
    y = jax.nn.gelu(y)
    o_ref[...] = (y * jax.nn.sigmoid(_dot(y.astype(BF16), wglu_ref[...]))).astype(BF16)


def _s5(u_tm, bblk, a_re, a_im, cblk, d, wglu, nb, tt=128):
    rows, w = u_tm.shape
    t = rows // nb
    blk = pl.BlockSpec((tt * nb, w), lambda i: (i, 0))
    return pl.pallas_call(
        functools.partial(_s5_body, tt=tt, nb=nb),
        grid=(t // tt,),
        in_specs=[blk, _const_spec(bblk.shape), _const_spec(a_re.shape), _const_spec(a_im.shape),
                  _const_spec(cblk.shape), _const_spec((1, w)), _const_spec(wglu.shape)],
        out_specs=blk, out_shape=jax.ShapeDtypeStruct((rows, w), BF16),
        scratch_shapes=[pltpu.VMEM((S5_NBLK, 2, nb, S5_HALF), F32),
                        pltpu.VMEM((tt * nb, 2 * S5_HALF), F32),
                        pltpu.VMEM((tt * nb, w), F32)],
        compiler_params=_params("arbitrary"), name="s5",
    )(u_tm, bblk, a_re, a_im, cblk, d.reshape(1, w), wglu)


GDN_ROWS = 128
GDN_HALO = SUBLANE


def _gdn_body(qkv_ref, z_ref, gate_ref, cw_ref, alog_ref, dtb_ref, ng_ref, o_ref,
              tail_ref, xpad_ref, s_ref, *, nbatch):
    n = GDN_ROWS

    @pl.when(pl.program_id(1) == 0)
    def _():
        tail_ref[...] = jnp.zeros_like(tail_ref)
        s_ref[...] = jnp.zeros_like(s_ref)

    def softplus(v):
        return jnp.maximum(v, 0.0) + jnp.log1p(jnp.exp(-jnp.abs(v)))

    ri = lax.broadcasted_iota(I32, (n, n), 0)
    ci = lax.broadcasted_iota(I32, (n, n), 1)
    lower = ri >= ci
    strict = ri > ci
    chunk_shift = CHUNK.bit_length() - 1
    same = lax.shift_right_logical(ri, chunk_shift) == lax.shift_right_logical(ci, chunk_shift)
    triu = jnp.where(ri <= ci, 1.0, 0.0).astype(F32)
    eye = jnp.where(ri == ci, 1.0, 0.0).astype(F32)
    eye_b = eye.astype(BF16)
    ones = jnp.ones((n, n), BF16)

    def row_sum(a):
        return _dot(a.astype(BF16), ones)

    def split3(a):
        hi = a.astype(BF16).astype(F32)
        mid = (a - hi).astype(BF16).astype(F32)
        return hi, mid, (a - hi - mid).astype(BF16).astype(F32)

    chains = [(bb, h) for bb in range(nbatch) for h in range(C_HEADS)]
    hs = lambda h: slice(h * C_HEAD_DIM, (h + 1) * C_HEAD_DIM)
    q, k, v, gc, gr, beta = {}, {}, {}, {}, {}, {}
    for bb in range(nbatch):
        xpad_ref[bb, 0:GDN_HALO, :] = tail_ref[bb]
        xpad_ref[bb, GDN_HALO:GDN_HALO + n, :] = qkv_ref[bb].astype(F32)
        tail_ref[bb] = xpad_ref[bb, n:n + GDN_HALO, :]
        xfull_rows = xpad_ref[bb]
        x = cw_ref[C_CONV - 1:C_CONV, :] * xfull_rows[GDN_HALO:GDN_HALO + n]
        for j in range(C_CONV - 1):
            shifted = pltpu.roll(xfull_rows, C_CONV - 1 - j, axis=0)
            x = x + cw_ref[j:j + 1, :] * shifted[GDN_HALO:GDN_HALO + n]
        x = _silu(x)
        gates = gate_ref[bb]
        g_rows = -jnp.exp(alog_ref[...]) * softplus(gates + dtb_ref[...])
        gam_rows = _dot_f32(g_rows, triu)
        row_terms = split3(jnp.where(lax.broadcasted_iota(I32, (16, n), 0) < SM_GB,
                                     gam_rows, jax.nn.sigmoid(gates)))
        for h in range(C_HEADS):
            c = (bb, h)
            spread = None
            for term in row_terms:
                rows = jnp.concatenate([jnp.broadcast_to(term[SM_GA + h:SM_GA + h + 1, :], (n, n)),
                                        jnp.broadcast_to(term[SM_GB + h:SM_GB + h + 1, :], (n, n))], axis=0)
                part = _dot_nt(eye_b, rows.astype(BF16))
                spread = part if spread is None else spread + part
            gc[c] = spread[:, :n]
            beta[c] = spread[:, n:]
            gr[c] = gam_rows[SM_GA + h:SM_GA + h + 1, :]
            qh = x[:, hs(h)]
            kh = x[:, C_WIDTH + h * C_HEAD_DIM:C_WIDTH + (h + 1) * C_HEAD_DIM]
            q[c] = qh * lax.rsqrt(row_sum(qh * qh) + EPS) * (C_HEAD_DIM ** -0.5)
            k[c] = kh * lax.rsqrt(row_sum(kh * kh) + EPS)
            v[c] = x[:, 2 * C_WIDTH + h * C_HEAD_DIM:2 * C_WIDTH + (h + 1) * C_HEAD_DIM]

    decay = {c: jnp.exp(jnp.where(lower, gc[c] - gr[c], MASKED)) for c in chains}
    kb = {c: k[c] * beta[c] for c in chains}
    kbf = {c: k[c].astype(BF16) for c in chains}
    m_full = {c: jnp.where(strict, _dot_nt(kb[c].astype(BF16), kbf[c]) * decay[c], 0.0) for c in chains}
    attn = {c: (_dot_nt(q[c].astype(BF16), kbf[c]) * decay[c]).astype(BF16) for c in chains}
    m_bd = {c: jnp.where(same, m_full[c], 0.0) for c in chains}
    m_off = {c: (m_full[c] - m_bd[c]).astype(BF16) for c in chains}
    xinv = {c: eye - m_bd[c] for c in chains}
    pw = {}
    for c in chains:
        mb = m_bd[c].astype(BF16)
        pw[c] = _dot(mb, mb)
    for lvl in range(5):
        pwb = {c: pw[c].astype(BF16) for c in chains}
        xinv = {c: xinv[c] + _dot(xinv[c].astype(BF16), pwb[c]) for c in chains}
        if lvl < 4:
            pw = {c: _dot(pwb[c], pwb[c]) for c in chains}
    xb = {c: xinv[c].astype(BF16) for c in chains}
    xm = {c: _dot(xb[c], m_off[c]).astype(BF16) for c in chains}
    xfull = {c: (xinv[c] - _dot(xm[c], xb[c])).astype(BF16) for c in chains}
    eg = {c: jnp.exp(gc[c]) for c in chains}
    sol = {c: _dot(xfull[c], jnp.concatenate([v[c] * beta[c], kb[c] * eg[c]], axis=1).astype(BF16))
           for c in chains}
    g_last = {c: gc[c][n - 1:n, :] for c in chains}
    q_dec = {c: (q[c] * eg[c]).astype(BF16) for c in chains}
    k_decT = {c: (k[c] * jnp.exp(g_last[c] - gc[c])).T.astype(BF16) for c in chains}
    s_old = {c: s_ref[c[0], c[1]] for c in chains}
    sb = {c: s_old[c].astype(BF16) for c in chains}
    vb = {c: (sol[c][:, :C_HEAD_DIM] - _dot(sol[c][:, C_HEAD_DIM:].astype(BF16), sb[c])).astype(BF16)
          for c in chains}
    o = {c: _dot(q_dec[c], sb[c]) + _dot(attn[c], vb[c]) for c in chains}
    for c in chains:
        bb, h = c
        s_ref[bb, h] = s_old[c] * jnp.exp(g_last[c]) + _dot(k_decT[c], vb[c])
        norm = lax.rsqrt(row_sum(o[c] * o[c]) * (1.0 / C_HEAD_DIM) + EPS)
        out = o[c] * norm * ng_ref[...] * _silu(z_ref[bb, :, hs(h)].astype(F32))
        o_ref[bb, :, hs(h)] = out.astype(BF16)


def _gdn(qkv, z, smT, conv_w, a_log, dt_bias, norm_g, nbatch=2):
    b, t, _ = qkv.shape
    n = GDN_ROWS
    assert b % nbatch == 0
    row = lambda w: pl.BlockSpec((nbatch, n, w), lambda bi, i: (bi, i, 0))
    pad_col = lambda v: jnp.zeros((16, 1), F32).at[SM_GA:SM_GA + C_HEADS, 0].set(v)
    return pl.pallas_call(
        functools.partial(_gdn_body, nbatch=nbatch),
        grid=(b // nbatch, t // n),
        in_specs=[row(3 * C_WIDTH), row(C_WIDTH),
                  pl.BlockSpec((nbatch, 16, n), lambda bi, i: (bi, 0, i)),
                  _const_spec((C_CONV, 3 * C_WIDTH)),
                  _const_spec((16, 1)), _const_spec((16, 1)), _const_spec((1, C_HEAD_DIM))],
        out_specs=row(C_WIDTH), out_shape=jax.ShapeDtypeStruct((b, t, C_WIDTH), BF16),
        scratch_shapes=[pltpu.VMEM((nbatch, GDN_HALO, 3 * C_WIDTH), F32),
                        pltpu.VMEM((nbatch, GDN_HALO + n, 3 * C_WIDTH), F32),
                        pltpu.VMEM((nbatch, C_HEADS, C_HEAD_DIM, C_HEAD_DIM), F32)],
        compiler_params=_params("parallel", "arbitrary"), name="gdn",
    )(qkv, z, smT, conv_w, pad_col(a_log), pad_col(dt_bias), norm_g.reshape(1, C_HEAD_DIM))


def _merge_body(x_ref, oa_ref, yb_ref, oc_ref, gate_ref, wpa_ref, wpb_ref, wpc_ref, wo_ref, o_ref):
    d = D_MODEL
    mix = jax.nn.sigmoid(gate_ref[:, 0:d].astype(F32)) * _dot(oa_ref[...], wpa_ref[...])
    mix = mix + jax.nn.sigmoid(gate_ref[:, d:2 * d].astype(F32)) * _dot(yb_ref[...], wpb_ref[...])
    mix = mix + jax.nn.sigmoid(gate_ref[:, 2 * d:3 * d].astype(F32)) * _dot(oc_ref[...], wpc_ref[...])
    o_ref[...] = x_ref[...] + _dot(mix.astype(BF16), wo_ref[...])


def _merge(x, oa, yb, oc, gates, wpa, wpb, wpc, wo, tm=512):
    b, t, d = x.shape
    row = lambda w: pl.BlockSpec((None, tm, w), lambda bi, i: (bi, i, 0))
    time_major = pl.BlockSpec((tm, S5_WIDTH), lambda bi, i: (i, bi))
    return pl.pallas_call(
        _merge_body,
        grid=(b, t // tm),
        in_specs=[row(d), row(A_WIDTH), time_major, row(C_WIDTH), row(N_BRANCH * d),
                  _const_spec(wpa.shape), _const_spec(wpb.shape), _const_spec(wpc.shape),
                  _const_spec(wo.shape)],
        out_specs=row(d), out_shape=jax.ShapeDtypeStruct((b, t, d), F32),
        compiler_params=_params("parallel", "parallel"), name="merge",
    )(x, oa, yb, oc, gates, wpa, wpb, wpc, wo)


FFN_TILE = 256
FFN_HALO = 2 * SUBLANE


def _ffn_body(x_ref, xprev_ref, g_ref, wup_ref, cw_ref, wdn_ref, fg_ref, o_ref, hcat_ref, hid_ref, acc_ref,
              *, tm, final_norm):
    first = pl.program_id(1) == 0
    hcat_ref[0:FFN_HALO, :] = _rms(xprev_ref[...], g_ref[...]).astype(BF16)
    hcat_ref[FFN_HALO:FFN_HALO + tm, :] = _rms(x_ref[...], g_ref[...]).astype(BF16)
    hb = hcat_ref[...]
    keep = jnp.where(jnp.logical_and(
        first, lax.broadcasted_iota(I32, (FFN_HALO + tm, 1), 0) < FFN_HALO), 0.0, 1.0)

    def conv(cols):
        hid_ref[...] = _dot(hb, wup_ref[:, cols]) * keep
        base = FFN_HALO - FFN_CONV + 1
        out = cw_ref[0:1, cols] * hid_ref[pl.ds(base, tm), :]
        for j in range(1, FFN_CONV):
            out = out + cw_ref[j:j + 1, cols] * hid_ref[pl.ds(base + j, tm), :]
        return out

    acc_ref[...] = jnp.zeros_like(acc_ref)
    for j in range(D_FF // FFN_TILE):
        gate = conv(slice(j * FFN_TILE, (j + 1) * FFN_TILE))
        val = conv(slice(D_FF + j * FFN_TILE, D_FF + (j + 1) * FFN_TILE))
        act = (_silu(gate) * val).astype(BF16)
        acc_ref[...] += _dot(act, wdn_ref[j * FFN_TILE:(j + 1) * FFN_TILE, :])
    y = x_ref[...] + acc_ref[...]
    if final_norm:
        y = _rms(y, fg_ref[...])
    o_ref[...] = y


def _ffn(x, g, wup, cw, wdn, final_g, final_norm, tm=512):
    b, t, d = x.shape
    row = pl.BlockSpec((None, tm, d), lambda bi, i: (bi, i, 0))
    halo_blocks = tm // FFN_HALO
    prev = pl.BlockSpec((None, FFN_HALO, d), lambda bi, i: (bi, jnp.maximum(i * halo_blocks - 1, 0), 0))
    return pl.pallas_call(
        functools.partial(_ffn_body, tm=tm, final_norm=final_norm),
        grid=(b, t // tm),
        in_specs=[row, prev, _const_spec((1, d)), _const_spec(wup.shape), _const_spec(cw.shape),
                  _const_spec(wdn.shape), _const_spec((1, d))],
        out_specs=row, out_shape=jax.ShapeDtypeStruct((b, t, d), F32),
        scratch_shapes=[pltpu.VMEM((FFN_HALO + tm, d), BF16),
                        pltpu.VMEM((FFN_HALO + tm, FFN_TILE), F32),
                        pltpu.VMEM((tm, d), F32)],
        compiler_params=_params("parallel", "parallel"), name="ffn",
    )(x, x, g.reshape(1, d), wup, cw, wdn, final_g.reshape(1, d))


def _pair_block_diag(w):
    h, r, c = w.shape
    w = w.reshape(h // 2, 2, r, c)
    z = jnp.zeros((h // 2, r, c), w.dtype)
    top = jnp.concatenate([w[:, 0], z], axis=2)
    bot = jnp.concatenate([z, w[:, 1]], axis=2)
    return jnp.concatenate([top, bot], axis=1)


def _transpose_body(w_ref, o_ref):
    o_ref[...] = w_ref[...].T.astype(BF16)


def _transpose_to_bf16(w):
    f, d = w.shape
    return pl.pallas_call(
        _transpose_body, grid=(f // LANE,),
        in_specs=[pl.BlockSpec((LANE, d), lambda i: (i, 0))],
        out_specs=pl.BlockSpec((d, LANE), lambda i: (0, i)),
        out_shape=jax.ShapeDtypeStruct((d, f), BF16),
        compiler_params=_params("parallel"), name="wn_transpose",
    )(w)


def _pack_in_proj(w_in):
    w_t = w_in.T
    d = w_t.shape[1]
    o = np.cumsum((0,) + IN_SPLITS)
    part = lambda i: w_t[o[i]:o[i + 1]]
    w_q, w_ckv, w_iq, w_ik, w_iw, w_s5, w_qkv, w_z, w_ca, w_cb, w_gate = (part(i) for i in range(11))
    iq = jnp.pad(w_iq.reshape(IDX_HEADS, IDX_DIM, d), ((0, 0), (0, LANE - IDX_DIM), (0, 0)))
    small = jnp.zeros((16, d), w_in.dtype)
    small = small.at[SM_IW:SM_IW + IDX_HEADS].set(w_iw).at[SM_GA:SM_GA + C_HEADS].set(w_ca)
    small = small.at[SM_GB:SM_GB + C_HEADS].set(w_cb)
    wt = jnp.concatenate([w_q, iq.reshape(IDX_HEADS * LANE, d), w_ckv, small], axis=0).astype(BF16)
    wn_rows = jnp.concatenate([w_ckv, jnp.pad(w_ik, ((0, LANE - IDX_DIM), (0, 0))), w_s5, w_qkv, w_z,
                               w_gate], axis=0)
    assert wt.shape[0] == WT_ROWS and wn_rows.shape[0] == WN_COLS
    return wt, _transpose_to_bf16(wn_rows)


def _pack_s5(bb_re, bb_im, ab_re, ab_im, c_re, c_im):
    eye = jnp.eye(S5_BLK, dtype=F32)
    def in_map(bbT):
        w = bbT.reshape(S5_NBLK, S5_BLK, S5_GROUP, S5_STATE)
        return jnp.einsum('jgmp,gh->jgmhp', w, eye).reshape(S5_NBLK, S5_BLK * S5_GROUP, S5_HALF)
    def out_map(c):
        w = c.reshape(S5_NBLK, S5_BLK, S5_GROUP, S5_STATE)
        return jnp.einsum('jgmp,gh->jgphm', w, eye).reshape(S5_NBLK, S5_HALF, S5_BLK * S5_GROUP)
    bblk = jnp.concatenate([in_map(bb_re), in_map(bb_im)], axis=2).astype(BF16)
    cblk = jnp.concatenate([out_map(c_re), out_map(-c_im)], axis=1).astype(BF16)
    return bblk, ab_re.reshape(S5_NBLK, 1, S5_HALF), ab_im.reshape(S5_NBLK, 1, S5_HALF), cblk


def kernel(x, attn_norm_g, w_in, kv_norm_g, w_uk, w_uv, w_proj_a, s5_a_re, s5_a_im, s5_log_dt, s5_b_re,
           s5_b_im, s5_c_re, s5_c_im, s5_d, w_glu, w_proj_b, gdn_conv_w, gdn_a_log, gdn_dt_bias,
           gdn_norm_g, w_proj_c, w_out, ffn_norm_g, w_up, ffn_conv_w, w_down, final_norm_g):
    b, t, d = x.shape
    depth = w_in.shape[0]
    for l in range(depth):
        wt, wn = _pack_in_proj(w_in[l])
        (qT, iqT, ckvT, smT, ckv, ik, u, qkv, z, gates) = _in_proj(x, attn_norm_g[l], wt, wn, kv_norm_g[l])

        wuk_bd = _pair_block_diag(
            jnp.transpose(w_uk[l], (1, 0, 2)) * (A_HEAD_DIM ** -0.5 * math.log2(math.e))).astype(BF16)
        wuv_bd = _pair_block_diag(jnp.transpose(w_uv[l], (1, 0, 2))).astype(BF16)
        oa = _dsa(qT, iqT, smT, ik, ckv, ckvT, wuk_bd, wuv_bd)

        ab_re, ab_im, bb_re, bb_im = _s5_discretise(
            s5_a_re[l], s5_a_im[l], s5_log_dt[l],
            jnp.transpose(s5_b_re[l], (0, 2, 1)), jnp.transpose(s5_b_im[l], (0, 2, 1)))
        bblk, a_r, a_i, cblk = _pack_s5(bb_re, bb_im, ab_re, ab_im, s5_c_re[l], s5_c_im[l])
        yb = _s5(u.reshape(t * b, S5_WIDTH), bblk, a_r, a_i, cblk, s5_d[l], w_glu[l].astype(BF16), nb=b)
        yb = yb.reshape(t, b * S5_WIDTH)

        oc = _gdn(qkv, z, smT, gdn_conv_w[l], gdn_a_log[l], gdn_dt_bias[l], gdn_norm_g[l])

        x = _merge(x, oa, yb, oc, gates, w_proj_a[l].astype(BF16), w_proj_b[l].astype(BF16),
                   w_proj_c[l].astype(BF16), w_out[l].astype(BF16))
        x = _ffn(x, ffn_norm_g[l], w_up[l].astype(BF16), ffn_conv_w[l], w_down[l].astype(BF16),
                 final_norm_g, final_norm=(l == depth - 1))
    return x
```

```python
import functools
import math

import numpy as np
import jax
import jax.numpy as jnp
from jax import lax
from jax.experimental import pallas as pl
from jax.experimental.pallas import tpu as pltpu

F32 = jnp.float32
BF16 = jnp.bfloat16
I32 = jnp.int32
I16 = jnp.int16

D_MODEL = 1024
EPS = 1e-6
CHUNK = 64
A_HEADS, A_HEAD_DIM, KV_RANK = 8, 64, 128
A_WIDTH = A_HEADS * A_HEAD_DIM
IDX_HEADS, IDX_DIM = 4, 64
TOPK_MAX = 256
S5_WIDTH, S5_GROUP, S5_STATE = 512, 16, 64
S5_GROUPS = S5_WIDTH // S5_GROUP
C_HEADS, C_HEAD_DIM, C_CONV = 4, 128, 4
C_WIDTH = C_HEADS * C_HEAD_DIM
D_FF, FFN_CONV = 2816, 3
N_BRANCH = 3
IN_SPLITS = (A_WIDTH, KV_RANK, IDX_HEADS * IDX_DIM, IDX_DIM, IDX_HEADS, S5_WIDTH,
             3 * C_WIDTH, C_WIDTH, C_HEADS, C_HEADS, N_BRANCH * D_MODEL)

LANE = 128
SUBLANE = 8
VMEM_LIMIT = 56 * 1024 * 1024
MASKED = -1e30
INT_MIN = -2 ** 31
DSA_KC = 512
NO_CUT = 2 ** 30
KV_EXT = KV_RANK + 16
HALF16 = 2 ** 15

WT_Q, WT_IQ, WT_CKV, WT_SMALL = 0, 512, 1024, 1152
WT_ROWS = 1168
SM_IW, SM_GA, SM_GB = 0, 8, 12
WN_CKV, WN_IK, WN_S5, WN_QKV, WN_Z, WN_GATE, WN_COLS = 0, 128, 256, 768, 2304, 2816, 5888


def _dot(a, b):
    return jnp.dot(a, b, preferred_element_type=F32)


def _dot_nt(a, b):
    return lax.dot_general(a, b, (((1,), (1,)), ((), ())), preferred_element_type=F32)


def _dot_f32(a, b):
    return jnp.dot(a, b, preferred_element_type=F32, precision=lax.Precision.HIGHEST)


def _rms(x, g):
    return x * lax.rsqrt(jnp.mean(x * x, axis=-1, keepdims=True) + EPS) * g


def _silu(x):
    return x * jax.nn.sigmoid(x)


def _const_spec(shape):
    nd = len(shape)
    return pl.BlockSpec(shape, lambda *_: (0,) * nd, pipeline_mode=pl.Buffered(1))


def _params(*sem):
    return pltpu.CompilerParams(dimension_semantics=sem, vmem_limit_bytes=VMEM_LIMIT)


def _in_proj_body(x_ref, g_ref, wt_ref, wn_ref, kvg_col_ref, kvg_row_ref,
                  qT_ref, iqT_ref, ckvT_ref, smT_ref, ckv_ref, ik_ref, u_ref, qkv_ref, z_ref,
                  gate_ref, *, tm):
    hb = _rms(x_ref[...], g_ref[...]).astype(BF16)
    qT_ref[...] = _dot_nt(wt_ref[WT_Q:WT_IQ, :], hb).astype(BF16)
    iqT_ref[...] = _dot_nt(wt_ref[WT_IQ:WT_CKV, :], hb).astype(BF16)
    cT = _dot_nt(wt_ref[WT_CKV:WT_SMALL, :], hb)
    cT = cT * lax.rsqrt(jnp.mean(cT * cT, axis=0, keepdims=True) + EPS) * kvg_col_ref[...]
    for j in range(tm // DSA_KC):
        ckvT_ref[j, 0:KV_RANK, :] = cT[:, j * DSA_KC:(j + 1) * DSA_KC].astype(BF16)
        ckvT_ref[j, KV_RANK:KV_EXT, :] = jnp.ones((KV_EXT - KV_RANK, DSA_KC), BF16)
    smT_ref[...] = _dot_nt(wt_ref[WT_SMALL:WT_ROWS, :], hb)
    c = _dot(hb, wn_ref[:, WN_CKV:WN_IK])
    ckv_ref[...] = _rms(c, kvg_row_ref[...]).astype(BF16)
    ik_ref[...] = _dot(hb, wn_ref[:, WN_IK:WN_S5]).astype(BF16)
    u_ref[...] = _dot(hb, wn_ref[:, WN_S5:WN_QKV]).astype(BF16)
    for j in range(3):
        lo = WN_QKV + j * C_WIDTH
        qkv_ref[:, j * C_WIDTH:(j + 1) * C_WIDTH] = _dot(hb, wn_ref[:, lo:lo + C_WIDTH]).astype(BF16)
    z_ref[...] = _dot(hb, wn_ref[:, WN_Z:WN_GATE]).astype(BF16)
    for j in range(2 * N_BRANCH):
        lo = WN_GATE + j * 512
        gate_ref[:, j * 512:(j + 1) * 512] = _dot(hb, wn_ref[:, lo:lo + 512]).astype(BF16)


def _in_proj(x, g, wt, wn, kvg, tm=DSA_KC):
    b, t, d = x.shape
    nt = t // tm
    row = lambda w: pl.BlockSpec((None, tm, w), lambda bi, i: (bi, i, 0))
    col = lambda r: pl.BlockSpec((None, r, tm), lambda bi, i: (bi, 0, i))
    out_shape = (
        jax.ShapeDtypeStruct((b, 512, t), BF16),
        jax.ShapeDtypeStruct((b, 512, t), BF16),
        jax.ShapeDtypeStruct((b, t // DSA_KC, KV_EXT, DSA_KC), BF16),
        jax.ShapeDtypeStruct((b, 16, t), F32),
        jax.ShapeDtypeStruct((b, t, KV_RANK), BF16),
        jax.ShapeDtypeStruct((b, t, LANE), BF16),
        jax.ShapeDtypeStruct((t, b * S5_WIDTH), BF16),
        jax.ShapeDtypeStruct((b, t, 3 * C_WIDTH), BF16),
        jax.ShapeDtypeStruct((b, t, C_WIDTH), BF16),
        jax.ShapeDtypeStruct((b, t, N_BRANCH * D_MODEL), BF16),
    )
    out_specs = (
        col(512), col(512),
        pl.BlockSpec((None, tm // DSA_KC, KV_EXT, DSA_KC), lambda bi, i: (bi, i, 0, 0)),
        col(16), row(KV_RANK), row(LANE),
        pl.BlockSpec((tm, S5_WIDTH), lambda bi, i: (i, bi)),
        row(3 * C_WIDTH), row(C_WIDTH), row(N_BRANCH * D_MODEL),
    )
    assert tm % DSA_KC == 0
    return pl.pallas_call(
        functools.partial(_in_proj_body, tm=tm),
        grid=(b, nt),
        in_specs=[row(d), _const_spec((1, d)), _const_spec(wt.shape), _const_spec(wn.shape),
                  _const_spec((KV_RANK, 1)), _const_spec((1, KV_RANK))],
        out_specs=out_specs, out_shape=out_shape,
        compiler_params=_params("parallel", "parallel"), name="in_proj",
    )(x, g.reshape(1, d), wt, wn, kvg.reshape(KV_RANK, 1), kvg.reshape(1, KV_RANK))


def _dsa_body(qT_ref, iqT_ref, smT_ref, ik_ref, ckv_ref, ckvT_ref, wuk_ref, wuv_ref, o_ref,
              key_ref, hi_ref, lo_ref, qlT_ref, lg_ref, acc_ref, sel_ref, *, topk, tq, idx_bits):
    kc_rows = DSA_KC
    hq = A_HEADS * tq
    qi = pl.program_id(1)
    n_kc = (qi * tq + tq + kc_rows - 1) // kc_rows

    def rows_of(kc):
        return pl.ds(pl.multiple_of(kc * kc_rows, kc_rows), kc_rows)

    for j in range(A_HEADS // 2):
        ql = _dot(wuk_ref[j], qT_ref[j * LANE:(j + 1) * LANE, :]).astype(BF16)
        qlT_ref[:, (2 * j) * tq:(2 * j + 1) * tq] = ql[0:KV_RANK]
        qlT_ref[:, (2 * j + 1) * tq:(2 * j + 2) * tq] = ql[KV_RANK:2 * KV_RANK]

    iw = smT_ref[SM_IW:SM_IW + IDX_HEADS, :] * (IDX_HEADS ** -0.5 * IDX_DIM ** -0.5)
    chunk_shift = CHUNK.bit_length() - 1
    q_chunk = lax.shift_right_logical(qi * tq + lax.broadcasted_iota(I32, (kc_rows, tq), 1), chunk_shift)
    row_iota = lax.broadcasted_iota(I32, (kc_rows, tq), 0)

    def score_body(kc, carry):
        ikc = ik_ref[rows_of(kc), :]
        sc = jnp.zeros((kc_rows, tq), F32)
        for h in range(IDX_HEADS):
            rel = jnp.maximum(_dot(ikc, iqT_ref[h * LANE:(h + 1) * LANE, :]), 0.0)
            sc = sc + rel * iw[h:h + 1, :]
        bits = lax.bitcast_convert_type(sc, I32)
        mag = bits & jnp.int32(0x7FFFFFFF)
        key = jnp.where(bits < 0, -mag, mag)
        adm = lax.shift_right_logical(row_iota + kc * kc_rows, chunk_shift) <= q_chunk
        key = jnp.where(adm, key, jnp.int32(INT_MIN))
        key_ref[rows_of(kc), :] = key
        hi_ref[rows_of(kc), :] = lax.shift_right_arithmetic(key, 16).astype(I16)
        lo_ref[rows_of(kc), :] = ((key & jnp.int32(0xFFFF)) - jnp.int32(HALF16)).astype(I16)
        return carry

    lax.fori_loop(0, n_kc, score_body, 0)

    def col_count(pred):
        lanes = 4 * SUBLANE
        def body(kc, acc):
            m = pred(key_ref[rows_of(kc), :], kc * kc_rows)
            mi = jnp.where(m, jnp.int32(1), jnp.int32(0))
            return acc + jnp.sum(mi.reshape(kc_rows // lanes, lanes, tq), axis=0)
        acc = lax.fori_loop(0, n_kc, body, jnp.zeros((lanes, tq), I32))
        return jnp.sum(acc, axis=0, keepdims=True)

    def col_count16(src_ref, pred):
        rows = 4 * SUBLANE
        def body(kc, acc):
            ones = jnp.where(pred(src_ref[rows_of(kc), :]), jnp.ones((), BF16), jnp.zeros((), BF16))
            parts = [ones[r * rows:(r + 1) * rows] for r in range(kc_rows // rows)]
            while len(parts) > 1:
                parts = [parts[i] + parts[i + 1] for i in range(0, len(parts), 2)]
            return acc + parts[0].astype(F32)
        acc = lax.fori_loop(0, n_kc, body, jnp.zeros((rows, tq), F32))
        return jnp.sum(acc, axis=0, keepdims=True).astype(I32)

    def kth_largest16(src_ref, rank):
        def bit_body(i, thr):
            cand = thr + lax.shift_left(jnp.int32(1), jnp.int32(15) - i)
            cand16 = cand.astype(I16)
            cnt = col_count16(src_ref, lambda v: v >= cand16)
            return jnp.where(cnt >= rank, cand, thr)
        return lax.fori_loop(0, 16, bit_body, jnp.full((1, tq), -HALF16, I32))

    sel_ref[0:1, :] = jnp.full((1, tq), INT_MIN, I32)
    sel_ref[1:2, :] = jnp.full((1, tq), -1, I32)

    @pl.when((qi + 1) * tq > topk)
    def _():
        thr_hi = kth_largest16(hi_ref, topk)
        thr_hi16 = thr_hi.astype(I16)
        rank_lo = topk - col_count16(hi_ref, lambda v: v > thr_hi16)

        def bucket_body(kc, carry):
            in_bucket = hi_ref[rows_of(kc), :] == thr_hi16
            lo_ref[rows_of(kc), :] = jnp.where(in_bucket, lo_ref[rows_of(kc), :], jnp.int16(-HALF16))
            return carry
        lax.fori_loop(0, n_kc, bucket_body, 0)
        thr_lo = kth_largest16(lo_ref, rank_lo)
        thr = lax.shift_left(thr_hi, 16) + (thr_lo + jnp.int32(HALF16))
        open_thr = thr != INT_MIN
        need = topk - col_count(lambda k, r0: k > thr)
        surplus = jnp.logical_and(col_count(lambda k, r0: k == thr) > need, open_thr)
        sel_ref[0:1, :] = thr
        sel_ref[1:2, :] = jnp.where(open_thr, jnp.int32(NO_CUT), jnp.int32(-1))

        @pl.when(jnp.max(jnp.where(surplus, 1, 0)) > 0)
        def _():
            def idx_body(i, jcut):
                cand = jcut + lax.shift_left(jnp.int32(1), jnp.int32(idx_bits - 1) - i)
                f = col_count(lambda k, r0: jnp.logical_and(k == thr, row_iota + r0 < cand))
                return jnp.where(f < need, cand, jcut)
            jcut = lax.fori_loop(0, idx_bits, idx_body, jnp.zeros((1, tq), I32))
            sel_ref[1:2, :] = jnp.where(open_thr, jcut, jnp.int32(-1))

    thr = sel_ref[0:1, :]
    jcut = sel_ref[1:2, :]

    def logit_body(kc, m8):
        key = key_ref[rows_of(kc), :]
        sel = jnp.logical_or(key > thr, jnp.logical_and(key == thr, row_iota + kc * kc_rows <= jcut))
        bias = jnp.where(sel, 0.0, MASKED)
        lg = _dot(ckv_ref[rows_of(kc), :], qlT_ref[...])
        parts = []
        for h in range(A_HEADS):
            x = lg[:, h * tq:(h + 1) * tq] + bias
            lg_ref[rows_of(kc), h * tq:(h + 1) * tq] = x
            parts.append(jnp.max(x.reshape(kc_rows // SUBLANE, SUBLANE, tq), axis=0))
        return jnp.maximum(m8, jnp.concatenate(parts, axis=1))

    m8 = lax.fori_loop(0, n_kc, logit_body, jnp.full((SUBLANE, hq), MASKED, F32))
    m = jnp.max(m8, axis=0, keepdims=True)

    acc_ref[...] = jnp.zeros_like(acc_ref)

    def pv_body(kc, carry):
        p = jnp.exp2(lg_ref[rows_of(kc), :] - m)
        acc_ref[...] += _dot(ckvT_ref[kc], p.astype(BF16))
        return carry

    lax.fori_loop(0, n_kc, pv_body, 0)
    l = acc_ref[KV_RANK:KV_RANK + 1, :]

    for j in range(A_HEADS // 2):
        parts = []
        for h in (2 * j, 2 * j + 1):
            o_latT = acc_ref[0:KV_RANK, h * tq:(h + 1) * tq] / l[:, h * tq:(h + 1) * tq]
            parts.append(o_latT.T)
        o_lat = jnp.concatenate(parts, axis=1).astype(BF16)
        o_ref[:, j * LANE:(j + 1) * LANE] = _dot(o_lat, wuv_ref[j]).astype(BF16)


def _dsa(qT, iqT, smT, ik, ckv, ckvT, wuk_bd, wuv_bd, tq=128):
    b, _, t = qT.shape
    assert t % DSA_KC == 0 and DSA_KC % tq == 0
    topk = min(TOPK_MAX, t // 4)
    nq = t // tq
    colq = lambda r: pl.BlockSpec((None, r, tq), lambda bi, i: (bi, 0, i))
    seq = lambda w: pl.BlockSpec((None, t, w), lambda bi, i: (bi, 0, 0))
    return pl.pallas_call(
        functools.partial(_dsa_body, topk=topk, tq=tq, idx_bits=int(t).bit_length()),
        grid=(b, nq),
        in_specs=[colq(512), colq(512), colq(16), seq(LANE), seq(KV_RANK),
                  pl.BlockSpec((None, t // DSA_KC, KV_EXT, DSA_KC), lambda bi, i: (bi, 0, 0, 0)),
                  _const_spec(wuk_bd.shape), _const_spec(wuv_bd.shape)],
        out_specs=pl.BlockSpec((None, tq, A_WIDTH), lambda bi, i: (bi, i, 0)),
        out_shape=jax.ShapeDtypeStruct((b, t, A_WIDTH), BF16),
        scratch_shapes=[pltpu.VMEM((t, tq), I32),
                        pltpu.VMEM((t, tq), I16),
                        pltpu.VMEM((t, tq), I16),
                        pltpu.VMEM((KV_RANK, A_HEADS * tq), BF16),
                        pltpu.VMEM((t, A_HEADS * tq), F32),
                        pltpu.VMEM((KV_EXT, A_HEADS * tq), F32),
                        pltpu.VMEM((SUBLANE, tq), I32)],
        compiler_params=_params("parallel", "arbitrary"), name="dsa",
    )(qT, iqT, smT, ik, ckv, ckvT, wuk_bd, wuv_bd)


def _s5_disc_body(are_ref, aim_ref, ldt_ref, bre_ref, bim_ref, abre_ref, abim_ref, bbre_ref, bbim_ref):
    a_re, a_im = are_ref[...], aim_ref[...]
    dt = jnp.exp(ldt_ref[...])
    mag = jnp.exp(dt * a_re)
    ang = dt * a_im
    ab_re = mag * jnp.cos(ang)
    ab_im = mag * jnp.sin(ang)
    den = a_re * a_re + a_im * a_im
    n_re = ab_re - 1.0
    f_re = (n_re * a_re + ab_im * a_im) / den
    f_im = (ab_im * a_re - n_re * a_im) / den
    abre_ref[...] = ab_re
    abim_ref[...] = ab_im
    b_re, b_im = bre_ref[...], bim_ref[...]
    bbre_ref[...] = f_re[:, None, :] * b_re - f_im[:, None, :] * b_im
    bbim_ref[...] = f_re[:, None, :] * b_im + f_im[:, None, :] * b_re


def _s5_discretise(a_re, a_im, log_dt, bT_re, bT_im):
    g, p = a_re.shape
    m = bT_re.shape[1]
    return pl.pallas_call(
        _s5_disc_body,
        out_shape=(jax.ShapeDtypeStruct((g, p), F32), jax.ShapeDtypeStruct((g, p), F32),
                   jax.ShapeDtypeStruct((g, m, p), F32), jax.ShapeDtypeStruct((g, m, p), F32)),
        name="s5_disc",
    )(a_re, a_im, log_dt.reshape(g, 1), bT_re, bT_im)


S5_BLK = 8
S5_NBLK = S5_GROUPS // S5_BLK
S5_HALF = S5_BLK * S5_STATE


def _s5_body(u_ref, bblk_ref, are_ref, aim_ref, cblk_ref, d_ref, wglu_ref, o_ref,
             hst_ref, xbuf_ref, y_ref, *, tt, nb):
    @pl.when(pl.program_id(0) == 0)
    def _():
        hst_ref[...] = jnp.zeros_like(hst_ref)

    for j in range(S5_NBLK):
        xbuf_ref[...] = _dot(u_ref[:, j * LANE:(j + 1) * LANE], bblk_ref[j])
        a_r = jnp.broadcast_to(are_ref[j], (nb, S5_HALF))
        a_i = jnp.broadcast_to(aim_ref[j], (nb, S5_HALF))

        def step(ti, carry):
            h_r, h_i = carry
            r0 = pl.multiple_of(ti * nb, nb)
            n_r = a_r * h_r - a_i * h_i + xbuf_ref[pl.ds(r0, nb), 0:S5_HALF]
            n_i = a_r * h_i + a_i * h_r + xbuf_ref[pl.ds(r0, nb), S5_HALF:2 * S5_HALF]
            xbuf_ref[pl.ds(r0, nb), 0:S5_HALF] = n_r
            xbuf_ref[pl.ds(r0, nb), S5_HALF:2 * S5_HALF] = n_i
            return n_r, n_i

        h_r, h_i = lax.fori_loop(0, tt, step, (hst_ref[j, 0], hst_ref[j, 1]), unroll=8)
        hst_ref[j, 0] = h_r
        hst_ref[j, 1] = h_i
        y_ref[:, j * LANE:(j + 1) * LANE] = _dot(xbuf_ref[...].astype(BF16), cblk_ref[j])

    y = y_ref[...] + d_ref[...] * u_ref[...].astype(F32)
    y = jax.nn.gelu(y)
    o_ref[...] = (y * jax.nn.sigmoid(_dot(y.astype(BF16), wglu_ref[...]))).astype(BF16)


def _s5(u_tm, bblk, a_re, a_im, cblk, d, wglu, nb, tt=128):
    rows, w = u_tm.shape
    t = rows // nb
    blk = pl.BlockSpec((tt * nb, w), lambda i: (i, 0))
    return pl.pallas_call(
        functools.partial(_s5_body, tt=tt, nb=nb),
        grid=(t // tt,),
        in_specs=[blk, _const_spec(bblk.shape), _const_spec(a_re.shape), _const_spec(a_im.shape),
                  _const_spec(cblk.shape), _const_spec((1, w)), _const_spec(wglu.shape)],
        out_specs=blk, out_shape=jax.ShapeDtypeStruct((rows, w), BF16),
        scratch_shapes=[pltpu.VMEM((S5_NBLK, 2, nb, S5_HALF), F32),
                        pltpu.VMEM((tt * nb, 2 * S5_HALF), F32),
                        pltpu.VMEM((tt * nb, w), F32)],
        compiler_params=_params("arbitrary"), name="s5",
    )(u_tm, bblk, a_re, a_im, cblk, d.reshape(1, w), wglu)


GDN_ROWS = 128
GDN_HALO = SUBLANE


def _gdn_body(qkv_ref, z_ref, gate_ref, cw_ref, alog_ref, dtb_ref, ng_ref, o_ref,
              tail_ref, xpad_ref, s_ref, *, nbatch):
    n = GDN_ROWS

    @pl.when(pl.program_id(1) == 0)
    def _():
        tail_ref[...] = jnp.zeros_like(tail_ref)
        s_ref[...] = jnp.zeros_like(s_ref)

    def softplus(v):
        return jnp.maximum(v, 0.0) + jnp.log1p(jnp.exp(-jnp.abs(v)))

    ri = lax.broadcasted_iota(I32, (n, n), 0)
    ci = lax.broadcasted_iota(I32, (n, n), 1)
    lower = ri >= ci
    strict = ri > ci
    chunk_shift = CHUNK.bit_length() - 1
    same = lax.shift_right_logical(ri, chunk_shift) == lax.shift_right_logical(ci, chunk_shift)
    triu = jnp.where(ri <= ci, 1.0, 0.0).astype(F32)
    eye = jnp.where(ri == ci, 1.0, 0.0).astype(F32)
    eye_b = eye.astype(BF16)
    ones = jnp.ones((n, n), BF16)

    def row_sum(a):
        return _dot(a.astype(BF16), ones)

    def split3(a):
        hi = a.astype(BF16).astype(F32)
        mid = (a - hi).astype(BF16).astype(F32)
        return hi, mid, (a - hi - mid).astype(BF16).astype(F32)

    chains = [(bb, h) for bb in range(nbatch) for h in range(C_HEADS)]
    hs = lambda h: slice(h * C_HEAD_DIM, (h + 1) * C_HEAD_DIM)
    q, k, v, gc, gr, beta = {}, {}, {}, {}, {}, {}
    for bb in range(nbatch):
        xpad_ref[bb, 0:GDN_HALO, :] = tail_ref[bb]
        xpad_ref[bb, GDN_HALO:GDN_HALO + n, :] = qkv_ref[bb].astype(F32)
        tail_ref[bb] = xpad_ref[bb, n:n + GDN_HALO, :]
        xfull_rows = xpad_ref[bb]
        x = cw_ref[C_CONV - 1:C_CONV, :] * xfull_rows[GDN_HALO:GDN_HALO + n]
        for j in range(C_CONV - 1):
            shifted = pltpu.roll(xfull_rows, C_CONV - 1 - j, axis=0)
            x = x + cw_ref[j:j + 1, :] * shifted[GDN_HALO:GDN_HALO + n]
        x = _silu(x)
        gates = gate_ref[bb]
        g_rows = -jnp.exp(alog_ref[...]) * softplus(gates + dtb_ref[...])
        gam_rows = _dot_f32(g_rows, triu)
        row_terms = split3(jnp.where(lax.broadcasted_iota(I32, (16, n), 0) < SM_GB,
                                     gam_rows, jax.nn.sigmoid(gates)))
        for h in range(C_HEADS):
            c = (bb, h)
            spread = None
            for term in row_terms:
                rows = jnp.concatenate([jnp.broadcast_to(term[SM_GA + h:SM_GA + h + 1, :], (n, n)),
                                        jnp.broadcast_to(term[SM_GB + h:SM_GB + h + 1, :], (n, n))], axis=0)
                part = _dot_nt(eye_b, rows.astype(BF16))
                spread = part if spread is None else spread + part
            gc[c] = spread[:, :n]
            beta[c] = spread[:, n:]
            gr[c] = gam_rows[SM_GA + h:SM_GA + h + 1, :]
            qh = x[:, hs(h)]
            kh = x[:, C_WIDTH + h * C_HEAD_DIM:C_WIDTH + (h + 1) * C_HEAD_DIM]
            q[c] = qh * lax.rsqrt(row_sum(qh * qh) + EPS) * (C_HEAD_DIM ** -0.5)
            k[c] = kh * lax.rsqrt(row_sum(kh * kh) + EPS)
            v[c] = x[:, 2 * C_WIDTH + h * C_HEAD_DIM:2 * C_WIDTH + (h + 1) * C_HEAD_DIM]

    decay = {c: jnp.exp(jnp.where(lower, gc[c] - gr[c], MASKED)) for c in chains}
    kb = {c: k[c] * beta[c] for c in chains}
    kbf = {c: k[c].astype(BF16) for c in chains}
    m_full = {c: jnp.where(strict, _dot_nt(kb[c].astype(BF16), kbf[c]) * decay[c], 0.0) for c in chains}
    attn = {c: (_dot_nt(q[c].astype(BF16), kbf[c]) * decay[c]).astype(BF16) for c in chains}
    m_bd = {c: jnp.where(same, m_full[c], 0.0) for c in chains}
    m_off = {c: (m_full[c] - m_bd[c]).astype(BF16) for c in chains}
    xinv = {c: eye - m_bd[c] for c in chains}
    pw = {}
    for c in chains:
        mb = m_bd[c].astype(BF16)
        pw[c] = _dot(mb, mb)
    for lvl in range(5):
        pwb = {c: pw[c].astype(BF16) for c in chains}
        xinv = {c: xinv[c] + _dot(xinv[c].astype(BF16), pwb[c]) for c in chains}
        if lvl < 4:
            pw = {c: _dot(pwb[c], pwb[c]) for c in chains}
    xb = {c: xinv[c].astype(BF16) for c in chains}
    xm = {c: _dot(xb[c], m_off[c]).astype(BF16) for c in chains}
    xfull = {c: (xinv[c] - _dot(xm[c], xb[c])).astype(BF16) for c in chains}
    eg = {c: jnp.exp(gc[c]) for c in chains}
    sol = {c: _dot(xfull[c], jnp.concatenate([v[c] * beta[c], kb[c] * eg[c]], axis=1).astype(BF16))
           for c in chains}
    g_last = {c: gc[c][n - 1:n, :] for c in chains}
    q_dec = {c: (q[c] * eg[c]).astype(BF16) for c in chains}
    k_decT = {c: (k[c] * jnp.exp(g_last[c] - gc[c])).T.astype(BF16) for c in chains}
    s_old = {c: s_ref[c[0], c[1]] for c in chains}
    sb = {c: s_old[c].astype(BF16) for c in chains}
    vb = {c: (sol[c][:, :C_HEAD_DIM] - _dot(sol[c][:, C_HEAD_DIM:].astype(BF16), sb[c])).astype(BF16)
          for c in chains}
    o = {c: _dot(q_dec[c], sb[c]) + _dot(attn[c], vb[c]) for c in chains}
    for c in chains:
        bb, h = c
        s_ref[bb, h] = s_old[c] * jnp.exp(g_last[c]) + _dot(k_decT[c], vb[c])
        norm = lax.rsqrt(row_sum(o[c] * o[c]) * (1.0 / C_HEAD_DIM) + EPS)
        out = o[c] * norm * ng_ref[...] * _silu(z_ref[bb, :, hs(h)].astype(F32))
        o_ref[bb, :, hs(h)] = out.astype(BF16)


def _gdn(qkv, z, smT, conv_w, a_log, dt_bias, norm_g, nbatch=2):
    b, t, _ = qkv.shape
    n = GDN_ROWS
    assert b % nbatch == 0
    row = lambda w: pl.BlockSpec((nbatch, n, w), lambda bi, i: (bi, i, 0))
    pad_col = lambda v: jnp.zeros((16, 1), F32).at[SM_GA:SM_GA + C_HEADS, 0].set(v)
    return pl.pallas_call(
        functools.partial(_gdn_body, nbatch=nbatch),
        grid=(b // nbatch, t // n),
        in_specs=[row(3 * C_WIDTH), row(C_WIDTH),
                  pl.BlockSpec((nbatch, 16, n), lambda bi, i: (bi, 0, i)),
                  _const_spec((C_CONV, 3 * C_WIDTH)),
                  _const_spec((16, 1)), _const_spec((16, 1)), _const_spec((1, C_HEAD_DIM))],
        out_specs=row(C_WIDTH), out_shape=jax.ShapeDtypeStruct((b, t, C_WIDTH), BF16),
        scratch_shapes=[pltpu.VMEM((nbatch, GDN_HALO, 3 * C_WIDTH), F32),
                        pltpu.VMEM((nbatch, GDN_HALO + n, 3 * C_WIDTH), F32),
                        pltpu.VMEM((nbatch, C_HEADS, C_HEAD_DIM, C_HEAD_DIM), F32)],
        compiler_params=_params("parallel", "arbitrary"), name="gdn",
    )(qkv, z, smT, conv_w, pad_col(a_log), pad_col(dt_bias), norm_g.reshape(1, C_HEAD_DIM))


def _merge_body(x_ref, oa_ref, yb_ref, oc_ref, gate_ref, wpa_ref, wpb_ref, wpc_ref, wo_ref, o_ref):
    d = D_MODEL
    mix = jax.nn.sigmoid(gate_ref[:, 0:d].astype(F32)) * _dot(oa_ref[...], wpa_ref[...])
    mix = mix + jax.nn.sigmoid(gate_ref[:, d:2 * d].astype(F32)) * _dot(yb_ref[...], wpb_ref[...])
    mix = mix + jax.nn.sigmoid(gate_ref[:, 2 * d:3 * d].astype(F32)) * _dot(oc_ref[...], wpc_ref[...])
    o_ref[...] = x_ref[...] + _dot(mix.astype(BF16), wo_ref[...])


def _merge(x, oa, yb, oc, gates, wpa, wpb, wpc, wo, tm=512):
    b, t, d = x.shape
    row = lambda w: pl.BlockSpec((None, tm, w), lambda bi, i: (bi, i, 0))
    time_major = pl.BlockSpec((tm, S5_WIDTH), lambda bi, i: (i, bi))
    return pl.pallas_call(
        _merge_body,
        grid=(b, t // tm),
        in_specs=[row(d), row(A_WIDTH), time_major, row(C_WIDTH), row(N_BRANCH * d),
                  _const_spec(wpa.shape), _const_spec(wpb.shape), _const_spec(wpc.shape),
                  _const_spec(wo.shape)],
        out_specs=row(d), out_shape=jax.ShapeDtypeStruct((b, t, d), F32),
        compiler_params=_params("parallel", "parallel"), name="merge",
    )(x, oa, yb, oc, gates, wpa, wpb, wpc, wo)


FFN_TILE = 256
FFN_HALO = 2 * SUBLANE


def _ffn_body(x_ref, xprev_ref, g_ref, wup_ref, cw_ref, wdn_ref, fg_ref, o_ref, hcat_ref, hid_ref, acc_ref,
              *, tm, final_norm):
    first = pl.program_id(1) == 0
    hcat_ref[0:FFN_HALO, :] = _rms(xprev_ref[...], g_ref[...]).astype(BF16)
    hcat_ref[FFN_HALO:FFN_HALO + tm, :] = _rms(x_ref[...], g_ref[...]).astype(BF16)
    hb = hcat_ref[...]
    keep = jnp.where(jnp.logical_and(
        first, lax.broadcasted_iota(I32, (FFN_HALO + tm, 1), 0) < FFN_HALO), 0.0, 1.0)

    n_tiles = D_FF // FFN_TILE
    gate_cols = lambda j: slice(j * FFN_TILE, (j + 1) * FFN_TILE)
    val_cols = lambda j: slice(D_FF + j * FFN_TILE, D_FF + (j + 1) * FFN_TILE)

    def project(j):
        hid_ref[2 * (j % 2)] = _dot(hb, wup_ref[:, gate_cols(j)]) * keep
        hid_ref[2 * (j % 2) + 1] = _dot(hb, wup_ref[:, val_cols(j)]) * keep

    def conv(cols, slot):
        base = FFN_HALO - FFN_CONV + 1
        out = cw_ref[0:1, cols] * hid_ref[slot, pl.ds(base, tm), :]
        for j in range(1, FFN_CONV):
            out = out + cw_ref[j:j + 1, cols] * hid_ref[slot, pl.ds(base + j, tm), :]
        return out

    acc_ref[...] = jnp.zeros_like(acc_ref)
    project(0)
    for j in range(n_tiles):
        if j + 1 < n_tiles:
            project(j + 1)
        gate = conv(gate_cols(j), 2 * (j % 2))
        val = conv(val_cols(j), 2 * (j % 2) + 1)
        act = (_silu(gate) * val).astype(BF16)
        acc_ref[...] += _dot(act, wdn_ref[j * FFN_TILE:(j + 1) * FFN_TILE, :])
    y = x_ref[...] + acc_ref[...]
    if final_norm:
        y = _rms(y, fg_ref[...])
    o_ref[...] = y


def _ffn(x, g, wup, cw, wdn, final_g, final_norm, tm=512):
    b, t, d = x.shape
    row = pl.BlockSpec((None, tm, d), lambda bi, i: (bi, i, 0))
    halo_blocks = tm // FFN_HALO
    prev = pl.BlockSpec((None, FFN_HALO, d), lambda bi, i: (bi, jnp.maximum(i * halo_blocks - 1, 0), 0))
    return pl.pallas_call(
        functools.partial(_ffn_body, tm=tm, final_norm=final_norm),
        grid=(b, t // tm),
        in_specs=[row, prev, _const_spec((1, d)), _const_spec(wup.shape), _const_spec(cw.shape),
                  _const_spec(wdn.shape), _const_spec((1, d))],
        out_specs=row, out_shape=jax.ShapeDtypeStruct((b, t, d), F32),
        scratch_shapes=[pltpu.VMEM((FFN_HALO + tm, d), BF16),
                        pltpu.VMEM((4, FFN_HALO + tm, FFN_TILE), F32),
                        pltpu.VMEM((tm, d), F32)],
        compiler_params=_params("parallel", "parallel"), name="ffn",
    )(x, x, g.reshape(1, d), wup, cw, wdn, final_g.reshape(1, d))


def _pair_block_diag(w):
    h, r, c = w.shape
    w = w.reshape(h // 2, 2, r, c)
    z = jnp.zeros((h // 2, r, c), w.dtype)
    top = jnp.concatenate([w[:, 0], z], axis=2)
    bot = jnp.concatenate([z, w[:, 1]], axis=2)
    return jnp.concatenate([top, bot], axis=1)


def _transpose_body(w_ref, o_ref):
    o_ref[...] = w_ref[...].T.astype(BF16)


def _transpose_to_bf16(w):
    f, d = w.shape
    return pl.pallas_call(
        _transpose_body, grid=(f // LANE,),
        in_specs=[pl.BlockSpec((LANE, d), lambda i: (i, 0))],
        out_specs=pl.BlockSpec((d, LANE), lambda i: (0, i)),
        out_shape=jax.ShapeDtypeStruct((d, f), BF16),
        compiler_params=_params("parallel"), name="wn_transpose",
    )(w)


def _pack_in_proj(w_in):
    w_t = w_in.T
    d = w_t.shape[1]
    o = np.cumsum((0,) + IN_SPLITS)
    part = lambda i: w_t[o[i]:o[i + 1]]
    w_q, w_ckv, w_iq, w_ik, w_iw, w_s5, w_qkv, w_z, w_ca, w_cb, w_gate = (part(i) for i in range(11))
    iq = jnp.pad(w_iq.reshape(IDX_HEADS, IDX_DIM, d), ((0, 0), (0, LANE - IDX_DIM), (0, 0)))
    small = jnp.zeros((16, d), w_in.dtype)
    small = small.at[SM_IW:SM_IW + IDX_HEADS].set(w_iw).at[SM_GA:SM_GA + C_HEADS].set(w_ca)
    small = small.at[SM_GB:SM_GB + C_HEADS].set(w_cb)
    wt = jnp.concatenate([w_q, iq.reshape(IDX_HEADS * LANE, d), w_ckv, small], axis=0).astype(BF16)
    wn_rows = jnp.concatenate([w_ckv, jnp.pad(w_ik, ((0, LANE - IDX_DIM), (0, 0))), w_s5, w_qkv, w_z,
                               w_gate], axis=0)
    assert wt.shape[0] == WT_ROWS and wn_rows.shape[0] == WN_COLS
    return wt, _transpose_to_bf16(wn_rows)


def _pack_s5(bb_re, bb_im, ab_re, ab_im, c_re, c_im):
    eye = jnp.eye(S5_BLK, dtype=F32)
    def in_map(bbT):
        w = bbT.reshape(S5_NBLK, S5_BLK, S5_GROUP, S5_STATE)
        return jnp.einsum('jgmp,gh->jgmhp', w, eye).reshape(S5_NBLK, S5_BLK * S5_GROUP, S5_HALF)
    def out_map(c):
        w = c.reshape(S5_NBLK, S5_BLK, S5_GROUP, S5_STATE)
        return jnp.einsum('jgmp,gh->jgphm', w, eye).reshape(S5_NBLK, S5_HALF, S5_BLK * S5_GROUP)
    bblk = jnp.concatenate([in_map(bb_re), in_map(bb_im)], axis=2).astype(BF16)
    cblk = jnp.concatenate([out_map(c_re), out_map(-c_im)], axis=1).astype(BF16)
    return bblk, ab_re.reshape(S5_NBLK, 1, S5_HALF), ab_im.reshape(S5_NBLK, 1, S5_HALF), cblk


def kernel(x, attn_norm_g, w_in, kv_norm_g, w_uk, w_uv, w_proj_a, s5_a_re, s5_a_im, s5_log_dt, s5_b_re,
           s5_b_im, s5_c_re, s5_c_im, s5_d, w_glu, w_proj_b, gdn_conv_w, gdn_a_log, gdn_dt_bias,
           gdn_norm_g, w_proj_c, w_out, ffn_norm_g, w_up, ffn_conv_w, w_down, final_norm_g):
    b, t, d = x.shape
    depth = w_in.shape[0]
    for l in range(depth):
        wt, wn = _pack_in_proj(w_in[l])
        (qT, iqT, ckvT, smT, ckv, ik, u, qkv, z, gates) = _in_proj(x, attn_norm_g[l], wt, wn, kv_norm_g[l])

        wuk_bd = _pair_block_diag(
            jnp.transpose(w_uk[l], (1, 0, 2)) * (A_HEAD_DIM ** -0.5 * math.log2(math.e))).astype(BF16)
        wuv_bd = _pair_block_diag(jnp.transpose(w_uv[l], (1, 0, 2))).astype(BF16)
        oa = _dsa(qT, iqT, smT, ik, ckv, ckvT, wuk_bd, wuv_bd)

        ab_re, ab_im, bb_re, bb_im = _s5_discretise(
            s5_a_re[l], s5_a_im[l], s5_log_dt[l],
            jnp.transpose(s5_b_re[l], (0, 2, 1)), jnp.transpose(s5_b_im[l], (0, 2, 1)))
        bblk, a_r, a_i, cblk = _pack_s5(bb_re, bb_im, ab_re, ab_im, s5_c_re[l], s5_c_im[l])
        yb = _s5(u.reshape(t * b, S5_WIDTH), bblk, a_r, a_i, cblk, s5_d[l], w_glu[l].astype(BF16), nb=b)
        yb = yb.reshape(t, b * S5_WIDTH)

        oc = _gdn(qkv, z, smT, gdn_conv_w[l], gdn_a_log[l], gdn_dt_bias[l], gdn_norm_g[l])

        x = _merge(x, oa, yb, oc, gates, w_proj_a[l].astype(BF16), w_proj_b[l].astype(BF16),
                   w_proj_c[l].astype(BF16), w_out[l].astype(BF16))
        x = _ffn(x, ffn_norm_g[l], w_up[l].astype(BF16), ffn_conv_w[l], w_down[l].astype(BF16),
                 final_norm_g, final_norm=(l == depth - 1))
    return x
```

```python
import functools
import math

import numpy as np
import jax
import jax.numpy as jnp
from jax import lax
from jax.experimental import pallas as pl
from jax.experimental.pallas import tpu as pltpu

F32 = jnp.float32
BF16 = jnp.bfloat16
I32 = jnp.int32
I16 = jnp.int16

D_MODEL = 1024
EPS = 1e-6
CHUNK = 64
A_HEADS, A_HEAD_DIM, KV_RANK = 8, 64, 128
A_WIDTH = A_HEADS * A_HEAD_DIM
IDX_HEADS, IDX_DIM = 4, 64
TOPK_MAX = 256
S5_WIDTH, S5_GROUP, S5_STATE = 512, 16, 64
S5_GROUPS = S5_WIDTH // S5_GROUP
C_HEADS, C_HEAD_DIM, C_CONV = 4, 128, 4
C_WIDTH = C_HEADS * C_HEAD_DIM
D_FF, FFN_CONV = 2816, 3
N_BRANCH = 3
IN_SPLITS = (A_WIDTH, KV_RANK, IDX_HEADS * IDX_DIM, IDX_DIM, IDX_HEADS, S5_WIDTH,
             3 * C_WIDTH, C_WIDTH, C_HEADS, C_HEADS, N_BRANCH * D_MODEL)

LANE = 128
SUBLANE = 8
VMEM_LIMIT = 56 * 1024 * 1024
MASKED = -1e30
INT_MIN = -2 ** 31
DSA_KC = 512
NO_CUT = 2 ** 30
KV_EXT = KV_RANK + 16
HALF16 = 2 ** 15

WT_Q, WT_IQ, WT_CKV, WT_SMALL = 0, 512, 1024, 1152
WT_ROWS = 1168
SM_IW, SM_GA, SM_GB = 0, 8, 12
WN_CKV, WN_IK, WN_S5, WN_QKV, WN_Z, WN_GATE, WN_COLS = 0, 128, 256, 768, 2304, 2816, 5888


def _dot(a, b):
    return jnp.dot(a, b, preferred_element_type=F32)


def _dot_nt(a, b):
    return lax.dot_general(a, b, (((1,), (1,)), ((), ())), preferred_element_type=F32)


def _dot_f32(a, b):
    return jnp.dot(a, b, preferred_element_type=F32, precision=lax.Precision.HIGHEST)


def _rms(x, g):
    return x * lax.rsqrt(jnp.mean(x * x, axis=-1, keepdims=True) + EPS) * g


def _silu(x):
    return x * jax.nn.sigmoid(x)


def _const_spec(shape):
    nd = len(shape)
    return pl.BlockSpec(shape, lambda *_: (0,) * nd, pipeline_mode=pl.Buffered(1))


def _params(*sem):
    return pltpu.CompilerParams(dimension_semantics=sem, vmem_limit_bytes=VMEM_LIMIT)


def _in_proj_body(x_ref, g_ref, wt_ref, wn_ref, kvg_col_ref, kvg_row_ref,
                  qT_ref, iqT_ref, ckvT_ref, smT_ref, ckv_ref, ik_ref, u_ref, qkv_ref, z_ref,
                  gate_ref, *, tm):
    hb = _rms(x_ref[...], g_ref[...]).astype(BF16)
    qT_ref[...] = _dot_nt(wt_ref[WT_Q:WT_IQ, :], hb).astype(BF16)
    iqT_ref[...] = _dot_nt(wt_ref[WT_IQ:WT_CKV, :], hb).astype(BF16)
    cT = _dot_nt(wt_ref[WT_CKV:WT_SMALL, :], hb)
    cT = cT * lax.rsqrt(jnp.mean(cT * cT, axis=0, keepdims=True) + EPS) * kvg_col_ref[...]
    for j in range(tm // DSA_KC):
        ckvT_ref[j, 0:KV_RANK, :] = cT[:, j * DSA_KC:(j + 1) * DSA_KC].astype(BF16)
        ckvT_ref[j, KV_RANK:KV_EXT, :] = jnp.ones((KV_EXT - KV_RANK, DSA_KC), BF16)
    smT_ref[...] = _dot_nt(wt_ref[WT_SMALL:WT_ROWS, :], hb)
    c = _dot(hb, wn_ref[:, WN_CKV:WN_IK])
    ckv_ref[...] = _rms(c, kvg_row_ref[...]).astype(BF16)
    ik_ref[...] = _dot(hb, wn_ref[:, WN_IK:WN_S5]).astype(BF16)
    u_ref[...] = _dot(hb, wn_ref[:, WN_S5:WN_QKV]).astype(BF16)
    for j in range(3):
        lo = WN_QKV + j * C_WIDTH
        qkv_ref[:, j * C_WIDTH:(j + 1) * C_WIDTH] = _dot(hb, wn_ref[:, lo:lo + C_WIDTH]).astype(BF16)
    z_ref[...] = _dot(hb, wn_ref[:, WN_Z:WN_GATE]).astype(BF16)
    for j in range(2 * N_BRANCH):
        lo = WN_GATE + j * 512
        gate_ref[:, j * 512:(j + 1) * 512] = _dot(hb, wn_ref[:, lo:lo + 512]).astype(BF16)


def _in_proj(x, g, wt, wn, kvg, tm=DSA_KC):
    b, t, d = x.shape
    nt = t // tm
    row = lambda w: pl.BlockSpec((None, tm, w), lambda bi, i: (bi, i, 0))
    col = lambda r: pl.BlockSpec((None, r, tm), lambda bi, i: (bi, 0, i))
    out_shape = (
        jax.ShapeDtypeStruct((b, 512, t), BF16),
        jax.ShapeDtypeStruct((b, 512, t), BF16),
        jax.ShapeDtypeStruct((b, t // DSA_KC, KV_EXT, DSA_KC), BF16),
        jax.ShapeDtypeStruct((b, 16, t), F32),
        jax.ShapeDtypeStruct((b, t, KV_RANK), BF16),
        jax.ShapeDtypeStruct((b, t, LANE), BF16),
        jax.ShapeDtypeStruct((t, b * S5_WIDTH), BF16),
        jax.ShapeDtypeStruct((b, t, 3 * C_WIDTH), BF16),
        jax.ShapeDtypeStruct((b, t, C_WIDTH), BF16),
        jax.ShapeDtypeStruct((b, t, N_BRANCH * D_MODEL), BF16),
    )
    out_specs = (
        col(512), col(512),
        pl.BlockSpec((None, tm // DSA_KC, KV_EXT, DSA_KC), lambda bi, i: (bi, i, 0, 0)),
        col(16), row(KV_RANK), row(LANE),
        pl.BlockSpec((tm, S5_WIDTH), lambda bi, i: (i, bi)),
        row(3 * C_WIDTH), row(C_WIDTH), row(N_BRANCH * D_MODEL),
    )
    assert tm % DSA_KC == 0
    return pl.pallas_call(
        functools.partial(_in_proj_body, tm=tm),
        grid=(b, nt),
        in_specs=[row(d), _const_spec((1, d)), _const_spec(wt.shape), _const_spec(wn.shape),
                  _const_spec((KV_RANK, 1)), _const_spec((1, KV_RANK))],
        out_specs=out_specs, out_shape=out_shape,
        compiler_params=_params("parallel", "parallel"), name="in_proj",
    )(x, g.reshape(1, d), wt, wn, kvg.reshape(KV_RANK, 1), kvg.reshape(1, KV_RANK))


def _dsa_body(qT_ref, iqT_ref, smT_ref, ik_ref, ckv_ref, ckvT_ref, wuk_ref, wuv_ref, o_ref,
              key_ref, hi_ref, lo_ref, qlT_ref, lg_ref, acc_ref, sel_ref, *, topk, tq, idx_bits):
    kc_rows = DSA_KC
    hq = A_HEADS * tq
    qi = pl.program_id(1)
    n_kc = (qi * tq + tq + kc_rows - 1) // kc_rows

    def rows_of(kc):
        return pl.ds(pl.multiple_of(kc * kc_rows, kc_rows), kc_rows)

    for j in range(A_HEADS // 2):
        ql = _dot(wuk_ref[j], qT_ref[j * LANE:(j + 1) * LANE, :]).astype(BF16)
        qlT_ref[:, (2 * j) * tq:(2 * j + 1) * tq] = ql[0:KV_RANK]
        qlT_ref[:, (2 * j + 1) * tq:(2 * j + 2) * tq] = ql[KV_RANK:2 * KV_RANK]

    iw = smT_ref[SM_IW:SM_IW + IDX_HEADS, :] * (IDX_HEADS ** -0.5 * IDX_DIM ** -0.5)
    chunk_shift = CHUNK.bit_length() - 1
    q_chunk = lax.shift_right_logical(qi * tq + lax.broadcasted_iota(I32, (kc_rows, tq), 1), chunk_shift)
    row_iota = lax.broadcasted_iota(I32, (kc_rows, tq), 0)

    def score_body(kc, carry):
        ikc = ik_ref[rows_of(kc), :]
        sc = jnp.zeros((kc_rows, tq), F32)
        for h in range(IDX_HEADS):
            rel = jnp.maximum(_dot(ikc, iqT_ref[h * LANE:(h + 1) * LANE, :]), 0.0)
            sc = sc + rel * iw[h:h + 1, :]
        bits = lax.bitcast_convert_type(sc, I32)
        mag = bits & jnp.int32(0x7FFFFFFF)
        key = jnp.where(bits < 0, -mag, mag)
        adm = lax.shift_right_logical(row_iota + kc * kc_rows, chunk_shift) <= q_chunk
        key = jnp.where(adm, key, jnp.int32(INT_MIN))
        key_ref[rows_of(kc), :] = key
        hi_ref[rows_of(kc), :] = lax.shift_right_arithmetic(key, 16).astype(I16)
        lo_ref[rows_of(kc), :] = ((key & jnp.int32(0xFFFF)) - jnp.int32(HALF16)).astype(I16)
        return carry

    lax.fori_loop(0, n_kc, score_body, 0)

    def col_count(pred):
        lanes = 4 * SUBLANE
        def body(kc, acc):
            m = pred(key_ref[rows_of(kc), :], kc * kc_rows)
            mi = jnp.where(m, jnp.int32(1), jnp.int32(0))
            return acc + jnp.sum(mi.reshape(kc_rows // lanes, lanes, tq), axis=0)
        acc = lax.fori_loop(0, n_kc, body, jnp.zeros((lanes, tq), I32))
        return jnp.sum(acc, axis=0, keepdims=True)

    def col_count16(src_ref, pred):
        rows = 4 * SUBLANE
        def body(kc, acc):
            ones = jnp.where(pred(src_ref[rows_of(kc), :]), jnp.ones((), BF16), jnp.zeros((), BF16))
            parts = [ones[r * rows:(r + 1) * rows] for r in range(kc_rows // rows)]
            while len(parts) > 1:
                parts = [parts[i] + parts[i + 1] for i in range(0, len(parts), 2)]
            return acc + parts[0].astype(F32)
        acc = lax.fori_loop(0, n_kc, body, jnp.zeros((rows, tq), F32))
        return jnp.sum(acc, axis=0, keepdims=True).astype(I32)

    def kth_largest16(src_ref, rank):
        def bit_body(i, thr):
            cand = thr + lax.shift_left(jnp.int32(1), jnp.int32(15) - i)
            cand16 = cand.astype(I16)
            cnt = col_count16(src_ref, lambda v: v >= cand16)
            return jnp.where(cnt >= rank, cand, thr)
        return lax.fori_loop(0, 16, bit_body, jnp.full((1, tq), -HALF16, I32))

    sel_ref[0:1, :] = jnp.full((1, tq), INT_MIN, I32)
    sel_ref[1:2, :] = jnp.full((1, tq), -1, I32)

    @pl.when((qi + 1) * tq > topk)
    def _():
        thr_hi = kth_largest16(hi_ref, topk)
        thr_hi16 = thr_hi.astype(I16)
        rank_lo = topk - col_count16(hi_ref, lambda v: v > thr_hi16)

        def bucket_body(kc, carry):
            in_bucket = hi_ref[rows_of(kc), :] == thr_hi16
            lo_ref[rows_of(kc), :] = jnp.where(in_bucket, lo_ref[rows_of(kc), :], jnp.int16(-HALF16))
            return carry
        lax.fori_loop(0, n_kc, bucket_body, 0)
        thr_lo = kth_largest16(lo_ref, rank_lo)
        thr = lax.shift_left(thr_hi, 16) + (thr_lo + jnp.int32(HALF16))
        open_thr = thr != INT_MIN
        need = topk - col_count(lambda k, r0: k > thr)
        surplus = jnp.logical_and(col_count(lambda k, r0: k == thr) > need, open_thr)
        sel_ref[0:1, :] = thr
        sel_ref[1:2, :] = jnp.where(open_thr, jnp.int32(NO_CUT), jnp.int32(-1))

        @pl.when(jnp.max(jnp.where(surplus, 1, 0)) > 0)
        def _():
            def idx_body(i, jcut):
                cand = jcut + lax.shift_left(jnp.int32(1), jnp.int32(idx_bits - 1) - i)
                f = col_count(lambda k, r0: jnp.logical_and(k == thr, row_iota + r0 < cand))
                return jnp.where(f < need, cand, jcut)
            jcut = lax.fori_loop(0, idx_bits, idx_body, jnp.zeros((1, tq), I32))
            sel_ref[1:2, :] = jnp.where(open_thr, jcut, jnp.int32(-1))

    thr = sel_ref[0:1, :]
    jcut = sel_ref[1:2, :]

    def logit_body(kc, m8):
        key = key_ref[rows_of(kc), :]
        sel = jnp.logical_or(key > thr, jnp.logical_and(key == thr, row_iota + kc * kc_rows <= jcut))
        bias = jnp.where(sel, 0.0, MASKED)
        lg = _dot(ckv_ref[rows_of(kc), :], qlT_ref[...])
        parts = []
        for h in range(A_HEADS):
            x = lg[:, h * tq:(h + 1) * tq] + bias
            lg_ref[rows_of(kc), h * tq:(h + 1) * tq] = x
            parts.append(jnp.max(x.reshape(kc_rows // SUBLANE, SUBLANE, tq), axis=0))
        return jnp.maximum(m8, jnp.concatenate(parts, axis=1))

    m8 = lax.fori_loop(0, n_kc, logit_body, jnp.full((SUBLANE, hq), MASKED, F32))
    m = jnp.max(m8, axis=0, keepdims=True)

    acc_ref[...] = jnp.zeros_like(acc_ref)

    def pv_body(kc, carry):
        p = jnp.exp2(lg_ref[rows_of(kc), :] - m)
        acc_ref[...] += _dot(ckvT_ref[kc], p.astype(BF16))
        return carry

    lax.fori_loop(0, n_kc, pv_body, 0)
    l = acc_ref[KV_RANK:KV_RANK + 1, :]

    for j in range(A_HEADS // 2):
        parts = []
        for h in (2 * j, 2 * j + 1):
            o_latT = acc_ref[0:KV_RANK, h * tq:(h + 1) * tq] / l[:, h * tq:(h + 1) * tq]
            parts.append(o_latT.T)
        o_lat = jnp.concatenate(parts, axis=1).astype(BF16)
        o_ref[:, j * LANE:(j + 1) * LANE] = _dot(o_lat, wuv_ref[j]).astype(BF16)


def _dsa(qT, iqT, smT, ik, ckv, ckvT, wuk_bd, wuv_bd, tq=128):
    b, _, t = qT.shape
    assert t % DSA_KC == 0 and DSA_KC % tq == 0
    topk = min(TOPK_MAX, t // 4)
    nq = t // tq
    colq = lambda r: pl.BlockSpec((None, r, tq), lambda bi, i: (bi, 0, i))
    seq = lambda w: pl.BlockSpec((None, t, w), lambda bi, i: (bi, 0, 0))
    return pl.pallas_call(
        functools.partial(_dsa_body, topk=topk, tq=tq, idx_bits=int(t).bit_length()),
        grid=(b, nq),
        in_specs=[colq(512), colq(512), colq(16), seq(LANE), seq(KV_RANK),
                  pl.BlockSpec((None, t // DSA_KC, KV_EXT, DSA_KC), lambda bi, i: (bi, 0, 0, 0)),
                  _const_spec(wuk_bd.shape), _const_spec(wuv_bd.shape)],
        out_specs=pl.BlockSpec((None, tq, A_WIDTH), lambda bi, i: (bi, i, 0)),
        out_shape=jax.ShapeDtypeStruct((b, t, A_WIDTH), BF16),
        scratch_shapes=[pltpu.VMEM((t, tq), I32),
                        pltpu.VMEM((t, tq), I16),
                        pltpu.VMEM((t, tq), I16),
                        pltpu.VMEM((KV_RANK, A_HEADS * tq), BF16),
                        pltpu.VMEM((t, A_HEADS * tq), F32),
                        pltpu.VMEM((KV_EXT, A_HEADS * tq), F32),
                        pltpu.VMEM((SUBLANE, tq), I32)],
        compiler_params=_params("parallel", "arbitrary"), name="dsa",
    )(qT, iqT, smT, ik, ckv, ckvT, wuk_bd, wuv_bd)


def _s5_disc_body(are_ref, aim_ref, ldt_ref, bre_ref, bim_ref, abre_ref, abim_ref, bbre_ref, bbim_ref):
    a_re, a_im = are_ref[...], aim_ref[...]
    dt = jnp.exp(ldt_ref[...])
    mag = jnp.exp(dt * a_re)
    ang = dt * a_im
    ab_re = mag * jnp.cos(ang)
    ab_im = mag * jnp.sin(ang)
    den = a_re * a_re + a_im * a_im
    n_re = ab_re - 1.0
    f_re = (n_re * a_re + ab_im * a_im) / den
    f_im = (ab_im * a_re - n_re * a_im) / den
    abre_ref[...] = ab_re
    abim_ref[...] = ab_im
    b_re, b_im = bre_ref[...], bim_ref[...]
    bbre_ref[...] = f_re[:, None, :] * b_re - f_im[:, None, :] * b_im
    bbim_ref[...] = f_re[:, None, :] * b_im + f_im[:, None, :] * b_re


def _s5_discretise(a_re, a_im, log_dt, bT_re, bT_im):
    g, p = a_re.shape
    m = bT_re.shape[1]
    return pl.pallas_call(
        _s5_disc_body,
        out_shape=(jax.ShapeDtypeStruct((g, p), F32), jax.ShapeDtypeStruct((g, p), F32),
                   jax.ShapeDtypeStruct((g, m, p), F32), jax.ShapeDtypeStruct((g, m, p), F32)),
        name="s5_disc",
    )(a_re, a_im, log_dt.reshape(g, 1), bT_re, bT_im)


S5_BLK = 8
S5_NBLK = S5_GROUPS // S5_BLK
S5_HALF = S5_BLK * S5_STATE


def _s5_body(u_ref, bblk_ref, are_ref, aim_ref, cblk_ref, d_ref, wglu_ref, o_ref,
             hst_ref, xbuf_ref, y_ref, *, tt, nb):
    @pl.when(pl.program_id(0) == 0)
    def _():
        hst_ref[...] = jnp.zeros_like(hst_ref)

    for j in range(S5_NBLK):
        xbuf_ref[...] = _dot(u_ref[:, j * LANE:(j + 1) * LANE], bblk_ref[j])
        a_r = jnp.broadcast_to(are_ref[j], (nb, S5_HALF))
        a_i = jnp.broadcast_to(aim_ref[j], (nb, S5_HALF))

        def step(ti, carry):
            h_r, h_i = carry
            r0 = pl.multiple_of(ti * nb, nb)
            n_r = a_r * h_r - a_i * h_i + xbuf_ref[pl.ds(r0, nb), 0:S5_HALF]
            n_i = a_r * h_i + a_i * h_r + xbuf_ref[pl.ds(r0, nb), S5_HALF:2 * S5_HALF]
            xbuf_ref[pl.ds(r0, nb), 0:S5_HALF] = n_r
            xbuf_ref[pl.ds(r0, nb), S5_HALF:2 * S5_HALF] = n_i
            return n_r, n_i

        h_r, h_i = lax.fori_loop(0, tt, step, (hst_ref[j, 0], hst_ref[j, 1]), unroll=8)
        hst_ref[j, 0] = h_r
        hst_ref[j, 1] = h_i
        y_ref[:, j * LANE:(j + 1) * LANE] = _dot(xbuf_ref[...].astype(BF16), cblk_ref[j])

    y = y_ref[...] + d_ref[...] * u_ref[...].astype(F32)
    y = jax.nn.gelu(y)
    o_ref[...] = (y * jax.nn.sigmoid(_dot(y.astype(BF16), wglu_ref[...]))).astype(BF16)


def _s5(u_tm, bblk, a_re, a_im, cblk, d, wglu, nb, tt=128):
    rows, w = u_tm.shape
    t = rows // nb
    blk = pl.BlockSpec((tt * nb, w), lambda i: (i, 0))
    return pl.pallas_call(
        functools.partial(_s5_body, tt=tt, nb=nb),
        grid=(t // tt,),
        in_specs=[blk, _const_spec(bblk.shape), _const_spec(a_re.shape), _const_spec(a_im.shape),
                  _const_spec(cblk.shape), _const_spec((1, w)), _const_spec(wglu.shape)],
        out_specs=blk, out_shape=jax.ShapeDtypeStruct((rows, w), BF16),
        scratch_shapes=[pltpu.VMEM((S5_NBLK, 2, nb, S5_HALF), F32),
                        pltpu.VMEM((tt * nb, 2 * S5_HALF), F32),
                        pltpu.VMEM((tt * nb, w), F32)],
        compiler_params=_params("arbitrary"), name="s5",
    )(u_tm, bblk, a_re, a_im, cblk, d.reshape(1, w), wglu)


GDN_ROWS = 128
GDN_HALO = SUBLANE


def _gdn_body(qkv_ref, z_ref, gate_ref, cw_ref, alog_ref, dtb_ref, ng_ref, o_ref,
              tail_ref, xpad_ref, s_ref, *, nbatch):
    n = GDN_ROWS

    @pl.when(pl.program_id(1) == 0)
    def _():
        tail_ref[...] = jnp.zeros_like(tail_ref)
        s_ref[...] = jnp.zeros_like(s_ref)

    def softplus(v):
        return jnp.maximum(v, 0.0) + jnp.log1p(jnp.exp(-jnp.abs(v)))

    ri = lax.broadcasted_iota(I32, (n, n), 0)
    ci = lax.broadcasted_iota(I32, (n, n), 1)
    lower = ri >= ci
    strict = ri > ci
    chunk_shift = CHUNK.bit_length() - 1
    same = lax.shift_right_logical(ri, chunk_shift) == lax.shift_right_logical(ci, chunk_shift)
    triu = jnp.where(ri <= ci, 1.0, 0.0).astype(F32)
    eye = jnp.where(ri == ci, 1.0, 0.0).astype(F32)
    eye_b = eye.astype(BF16)
    ones = jnp.ones((n, n), BF16)

    def row_sum(a):
        return _dot(a.astype(BF16), ones)

    def split3(a):
        hi = a.astype(BF16).astype(F32)
        mid = (a - hi).astype(BF16).astype(F32)
        return hi, mid, (a - hi - mid).astype(BF16).astype(F32)

    chains = [(bb, h) for bb in range(nbatch) for h in range(C_HEADS)]
    hs = lambda h: slice(h * C_HEAD_DIM, (h + 1) * C_HEAD_DIM)
    q, k, v, gc, gr, beta = {}, {}, {}, {}, {}, {}
    for bb in range(nbatch):
        xpad_ref[bb, 0:GDN_HALO, :] = tail_ref[bb]
        xpad_ref[bb, GDN_HALO:GDN_HALO + n, :] = qkv_ref[bb].astype(F32)
        tail_ref[bb] = xpad_ref[bb, n:n + GDN_HALO, :]
        xfull_rows = xpad_ref[bb]
        x = cw_ref[C_CONV - 1:C_CONV, :] * xfull_rows[GDN_HALO:GDN_HALO + n]
        for j in range(C_CONV - 1):
            shifted = pltpu.roll(xfull_rows, C_CONV - 1 - j, axis=0)
            x = x + cw_ref[j:j + 1, :] * shifted[GDN_HALO:GDN_HALO + n]
        x = _silu(x)
        gates = gate_ref[bb]
        g_rows = -jnp.exp(alog_ref[...]) * softplus(gates + dtb_ref[...])
        gam_rows = _dot_f32(g_rows, triu)
        row_terms = split3(jnp.where(lax.broadcasted_iota(I32, (16, n), 0) < SM_GB,
                                     gam_rows, jax.nn.sigmoid(gates)))
        for h in range(C_HEADS):
            c = (bb, h)
            spread = None
            for term in row_terms:
                rows = jnp.concatenate([jnp.broadcast_to(term[SM_GA + h:SM_GA + h + 1, :], (n, n)),
                                        jnp.broadcast_to(term[SM_GB + h:SM_GB + h + 1, :], (n, n))], axis=0)
                part = _dot_nt(eye_b, rows.astype(BF16))
                spread = part if spread is None else spread + part
            gc[c] = spread[:, :n]
            beta[c] = spread[:, n:]
            gr[c] = gam_rows[SM_GA + h:SM_GA + h + 1, :]
            qh = x[:, hs(h)]
            kh = x[:, C_WIDTH + h * C_HEAD_DIM:C_WIDTH + (h + 1) * C_HEAD_DIM]
            q[c] = qh * lax.rsqrt(row_sum(qh * qh) + EPS) * (C_HEAD_DIM ** -0.5)
            k[c] = kh * lax.rsqrt(row_sum(kh * kh) + EPS)
            v[c] = x[:, 2 * C_WIDTH + h * C_HEAD_DIM:2 * C_WIDTH + (h + 1) * C_HEAD_DIM]

    decay = {c: jnp.exp(jnp.where(lower, gc[c] - gr[c], MASKED)) for c in chains}
    kb = {c: k[c] * beta[c] for c in chains}
    kbf = {c: k[c].astype(BF16) for c in chains}
    m_full = {c: jnp.where(strict, _dot_nt(kb[c].astype(BF16), kbf[c]) * decay[c], 0.0) for c in chains}
    attn = {c: (_dot_nt(q[c].astype(BF16), kbf[c]) * decay[c]).astype(BF16) for c in chains}
    m_bd = {c: jnp.where(same, m_full[c], 0.0) for c in chains}
    m_off = {c: (m_full[c] - m_bd[c]).astype(BF16) for c in chains}
    xinv = {c: eye - m_bd[c] for c in chains}
    pw = {}
    for c in chains:
        mb = m_bd[c].astype(BF16)
        pw[c] = _dot(mb, mb)
    for lvl in range(5):
        pwb = {c: pw[c].astype(BF16) for c in chains}
        xinv = {c: xinv[c] + _dot(xinv[c].astype(BF16), pwb[c]) for c in chains}
        if lvl < 4:
            pw = {c: _dot(pwb[c], pwb[c]) for c in chains}
    xb = {c: xinv[c].astype(BF16) for c in chains}
    xm = {c: _dot(xb[c], m_off[c]).astype(BF16) for c in chains}
    xfull = {c: (xinv[c] - _dot(xm[c], xb[c])).astype(BF16) for c in chains}
    eg = {c: jnp.exp(gc[c]) for c in chains}
    sol = {c: _dot(xfull[c], jnp.concatenate([v[c] * beta[c], kb[c] * eg[c]], axis=1).astype(BF16))
           for c in chains}
    g_last = {c: gc[c][n - 1:n, :] for c in chains}
    q_dec = {c: (q[c] * eg[c]).astype(BF16) for c in chains}
    k_decT = {c: (k[c] * jnp.exp(g_last[c] - gc[c])).T.astype(BF16) for c in chains}
    s_old = {c: s_ref[c[0], c[1]] for c in chains}
    sb = {c: s_old[c].astype(BF16) for c in chains}
    vb = {c: (sol[c][:, :C_HEAD_DIM] - _dot(sol[c][:, C_HEAD_DIM:].astype(BF16), sb[c])).astype(BF16)
          for c in chains}
    o = {c: _dot(q_dec[c], sb[c]) + _dot(attn[c], vb[c]) for c in chains}
    for c in chains:
        bb, h = c
        s_ref[bb, h] = s_old[c] * jnp.exp(g_last[c]) + _dot(k_decT[c], vb[c])
        norm = lax.rsqrt(row_sum(o[c] * o[c]) * (1.0 / C_HEAD_DIM) + EPS)
        out = o[c] * norm * ng_ref[...] * _silu(z_ref[bb, :, hs(h)].astype(F32))
        o_ref[bb, :, hs(h)] = out.astype(BF16)


def _gdn(qkv, z, smT, conv_w, a_log, dt_bias, norm_g, nbatch=4):
    b, t, _ = qkv.shape
    n = GDN_ROWS
    assert b % nbatch == 0
    row = lambda w: pl.BlockSpec((nbatch, n, w), lambda bi, i: (bi, i, 0))
    pad_col = lambda v: jnp.zeros((16, 1), F32).at[SM_GA:SM_GA + C_HEADS, 0].set(v)
    return pl.pallas_call(
        functools.partial(_gdn_body, nbatch=nbatch),
        grid=(b // nbatch, t // n),
        in_specs=[row(3 * C_WIDTH), row(C_WIDTH),
                  pl.BlockSpec((nbatch, 16, n), lambda bi, i: (bi, 0, i)),
                  _const_spec((C_CONV, 3 * C_WIDTH)),
                  _const_spec((16, 1)), _const_spec((16, 1)), _const_spec((1, C_HEAD_DIM))],
        out_specs=row(C_WIDTH), out_shape=jax.ShapeDtypeStruct((b, t, C_WIDTH), BF16),
        scratch_shapes=[pltpu.VMEM((nbatch, GDN_HALO, 3 * C_WIDTH), F32),
                        pltpu.VMEM((nbatch, GDN_HALO + n, 3 * C_WIDTH), F32),
                        pltpu.VMEM((nbatch, C_HEADS, C_HEAD_DIM, C_HEAD_DIM), F32)],
        compiler_params=_params("parallel", "arbitrary"), name="gdn",
    )(qkv, z, smT, conv_w, pad_col(a_log), pad_col(dt_bias), norm_g.reshape(1, C_HEAD_DIM))


def _merge_body(x_ref, oa_ref, yb_ref, oc_ref, gate_ref, wpa_ref, wpb_ref, wpc_ref, wo_ref, o_ref):
    d = D_MODEL
    mix = jax.nn.sigmoid(gate_ref[:, 0:d].astype(F32)) * _dot(oa_ref[...], wpa_ref[...])
    mix = mix + jax.nn.sigmoid(gate_ref[:, d:2 * d].astype(F32)) * _dot(yb_ref[...], wpb_ref[...])
    mix = mix + jax.nn.sigmoid(gate_ref[:, 2 * d:3 * d].astype(F32)) * _dot(oc_ref[...], wpc_ref[...])
    o_ref[...] = x_ref[...] + _dot(mix.astype(BF16), wo_ref[...])


def _merge(x, oa, yb, oc, gates, wpa, wpb, wpc, wo, tm=512):
    b, t, d = x.shape
    row = lambda w: pl.BlockSpec((None, tm, w), lambda bi, i: (bi, i, 0))
    time_major = pl.BlockSpec((tm, S5_WIDTH), lambda bi, i: (i, bi))
    return pl.pallas_call(
        _merge_body,
        grid=(b, t // tm),
        in_specs=[row(d), row(A_WIDTH), time_major, row(C_WIDTH), row(N_BRANCH * d),
                  _const_spec(wpa.shape), _const_spec(wpb.shape), _const_spec(wpc.shape),
                  _const_spec(wo.shape)],
        out_specs=row(d), out_shape=jax.ShapeDtypeStruct((b, t, d), F32),
        compiler_params=_params("parallel", "parallel"), name="merge",
    )(x, oa, yb, oc, gates, wpa, wpb, wpc, wo)


FFN_TILE = 256
FFN_HALO = 2 * SUBLANE


def _ffn_body(x_ref, xprev_ref, g_ref, wup_ref, cw_ref, wdn_ref, fg_ref, o_ref, hcat_ref, hid_ref, acc_ref,
              *, tm, final_norm):
    first = pl.program_id(1) == 0
    hcat_ref[0:FFN_HALO, :] = _rms(xprev_ref[...], g_ref[...]).astype(BF16)
    hcat_ref[FFN_HALO:FFN_HALO + tm, :] = _rms(x_ref[...], g_ref[...]).astype(BF16)
    hb = hcat_ref[...]
    keep = jnp.where(jnp.logical_and(
        first, lax.broadcasted_iota(I32, (FFN_HALO + tm, 1), 0) < FFN_HALO), 0.0, 1.0)

    n_tiles = D_FF // FFN_TILE
    gate_cols = lambda j: slice(j * FFN_TILE, (j + 1) * FFN_TILE)
    val_cols = lambda j: slice(D_FF + j * FFN_TILE, D_FF + (j + 1) * FFN_TILE)

    def project(j):
        hid_ref[2 * (j % 2)] = _dot(hb, wup_ref[:, gate_cols(j)]) * keep
        hid_ref[2 * (j % 2) + 1] = _dot(hb, wup_ref[:, val_cols(j)]) * keep

    def conv(cols, slot):
        base = FFN_HALO - FFN_CONV + 1
        out = cw_ref[0:1, cols] * hid_ref[slot, pl.ds(base, tm), :]
        for j in range(1, FFN_CONV):
            out = out + cw_ref[j:j + 1, cols] * hid_ref[slot, pl.ds(base + j, tm), :]
        return out

    acc_ref[...] = jnp.zeros_like(acc_ref)
    project(0)
    for j in range(n_tiles):
        if j + 1 < n_tiles:
            project(j + 1)
        gate = conv(gate_cols(j), 2 * (j % 2))
        val = conv(val_cols(j), 2 * (j % 2) + 1)
        act = (_silu(gate) * val).astype(BF16)
        acc_ref[...] += _dot(act, wdn_ref[j * FFN_TILE:(j + 1) * FFN_TILE, :])
    y = x_ref[...] + acc_ref[...]
    if final_norm:
        y = _rms(y, fg_ref[...])
    o_ref[...] = y


def _ffn(x, g, wup, cw, wdn, final_g, final_norm, tm=512):
    b, t, d = x.shape
    row = pl.BlockSpec((None, tm, d), lambda bi, i: (bi, i, 0))
    halo_blocks = tm // FFN_HALO
    prev = pl.BlockSpec((None, FFN_HALO, d), lambda bi, i: (bi, jnp.maximum(i * halo_blocks - 1, 0), 0))
    return pl.pallas_call(
        functools.partial(_ffn_body, tm=tm, final_norm=final_norm),
        grid=(b, t // tm),
        in_specs=[row, prev, _const_spec((1, d)), _const_spec(wup.shape), _const_spec(cw.shape),
                  _const_spec(wdn.shape), _const_spec((1, d))],
        out_specs=row, out_shape=jax.ShapeDtypeStruct((b, t, d), F32),
        scratch_shapes=[pltpu.VMEM((FFN_HALO + tm, d), BF16),
                        pltpu.VMEM((4, FFN_HALO + tm, FFN_TILE), F32),
                        pltpu.VMEM((tm, d), F32)],
        compiler_params=_params("parallel", "parallel"), name="ffn",
    )(x, x, g.reshape(1, d), wup, cw, wdn, final_g.reshape(1, d))


def _pair_block_diag(w):
    h, r, c = w.shape
    w = w.reshape(h // 2, 2, r, c)
    z = jnp.zeros((h // 2, r, c), w.dtype)
    top = jnp.concatenate([w[:, 0], z], axis=2)
    bot = jnp.concatenate([z, w[:, 1]], axis=2)
    return jnp.concatenate([top, bot], axis=1)


def _transpose_body(w_ref, o_ref):
    o_ref[...] = w_ref[...].T.astype(BF16)


def _transpose_to_bf16(w):
    f, d = w.shape
    return pl.pallas_call(
        _transpose_body, grid=(f // LANE,),
        in_specs=[pl.BlockSpec((LANE, d), lambda i: (i, 0))],
        out_specs=pl.BlockSpec((d, LANE), lambda i: (0, i)),
        out_shape=jax.ShapeDtypeStruct((d, f), BF16),
        compiler_params=_params("parallel"), name="wn_transpose",
    )(w)


def _pack_in_proj(w_in):
    w_t = w_in.T
    d = w_t.shape[1]
    o = np.cumsum((0,) + IN_SPLITS)
    part = lambda i: w_t[o[i]:o[i + 1]]
    w_q, w_ckv, w_iq, w_ik, w_iw, w_s5, w_qkv, w_z, w_ca, w_cb, w_gate = (part(i) for i in range(11))
    iq = jnp.pad(w_iq.reshape(IDX_HEADS, IDX_DIM, d), ((0, 0), (0, LANE - IDX_DIM), (0, 0)))
    small = jnp.zeros((16, d), w_in.dtype)
    small = small.at[SM_IW:SM_IW + IDX_HEADS].set(w_iw).at[SM_GA:SM_GA + C_HEADS].set(w_ca)
    small = small.at[SM_GB:SM_GB + C_HEADS].set(w_cb)
    wt = jnp.concatenate([w_q, iq.reshape(IDX_HEADS * LANE, d), w_ckv, small], axis=0).astype(BF16)
    wn_rows = jnp.concatenate([w_ckv, jnp.pad(w_ik, ((0, LANE - IDX_DIM), (0, 0))), w_s5, w_qkv, w_z,
                               w_gate], axis=0)
    assert wt.shape[0] == WT_ROWS and wn_rows.shape[0] == WN_COLS
    return wt, _transpose_to_bf16(wn_rows)


def _pack_s5(bb_re, bb_im, ab_re, ab_im, c_re, c_im):
    eye = jnp.eye(S5_BLK, dtype=F32)
    def in_map(bbT):
        w = bbT.reshape(S5_NBLK, S5_BLK, S5_GROUP, S5_STATE)
        return jnp.einsum('jgmp,gh->jgmhp', w, eye).reshape(S5_NBLK, S5_BLK * S5_GROUP, S5_HALF)
    def out_map(c):
        w = c.reshape(S5_NBLK, S5_BLK, S5_GROUP, S5_STATE)
        return jnp.einsum('jgmp,gh->jgphm', w, eye).reshape(S5_NBLK, S5_HALF, S5_BLK * S5_GROUP)
    bblk = jnp.concatenate([in_map(bb_re), in_map(bb_im)], axis=2).astype(BF16)
    cblk = jnp.concatenate([out_map(c_re), out_map(-c_im)], axis=1).astype(BF16)
    return bblk, ab_re.reshape(S5_NBLK, 1, S5_HALF), ab_im.reshape(S5_NBLK, 1, S5_HALF), cblk


def kernel(x, attn_norm_g, w_in, kv_norm_g, w_uk, w_uv, w_proj_a, s5_a_re, s5_a_im, s5_log_dt, s5_b_re,
           s5_b_im, s5_c_re, s5_c_im, s5_d, w_glu, w_proj_b, gdn_conv_w, gdn_a_log, gdn_dt_bias,
           gdn_norm_g, w_proj_c, w_out, ffn_norm_g, w_up, ffn_conv_w, w_down, final_norm_g):
    b, t, d = x.shape
    depth = w_in.shape[0]
    for l in range(depth):
        wt, wn = _pack_in_proj(w_in[l])
        (qT, iqT, ckvT, smT, ckv, ik, u, qkv, z, gates) = _in_proj(x, attn_norm_g[l], wt, wn, kv_norm_g[l])

        wuk_bd = _pair_block_diag(
            jnp.transpose(w_uk[l], (1, 0, 2)) * (A_HEAD_DIM ** -0.5 * math.log2(math.e))).astype(BF16)
        wuv_bd = _pair_block_diag(jnp.transpose(w_uv[l], (1, 0, 2))).astype(BF16)
        oa = _dsa(qT, iqT, smT, ik, ckv, ckvT, wuk_bd, wuv_bd)

        ab_re, ab_im, bb_re, bb_im = _s5_discretise(
            s5_a_re[l], s5_a_im[l], s5_log_dt[l],
            jnp.transpose(s5_b_re[l], (0, 2, 1)), jnp.transpose(s5_b_im[l], (0, 2, 1)))
        bblk, a_r, a_i, cblk = _pack_s5(bb_re, bb_im, ab_re, ab_im, s5_c_re[l], s5_c_im[l])
        yb = _s5(u.reshape(t * b, S5_WIDTH), bblk, a_r, a_i, cblk, s5_d[l], w_glu[l].astype(BF16), nb=b)
        yb = yb.reshape(t, b * S5_WIDTH)

        oc = _gdn(qkv, z, smT, gdn_conv_w[l], gdn_a_log[l], gdn_dt_bias[l], gdn_norm_g[l])

        x = _merge(x, oa, yb, oc, gates, w_proj_a[l].astype(BF16), w_proj_b[l].astype(BF16),
                   w_proj_c[l].astype(BF16), w_out[l].astype(BF16))
        x = _ffn(x, ffn_norm_g[l], w_up[l].astype(BF16), ffn_conv_w[l], w_down[l].astype(BF16),
                 final_norm_g, final_norm=(l == depth - 1))
    return x
```
